```python
import jax
import jax.numpy as jnp
from jax import lax
import numpy as np

D_MODEL = 1024
BATCH = 4
SEQ = 4096
DEPTH = 2
DEC_BATCH = 128
DEC_SEQ = 4
PAST_LEN = 8192
PAGE_SIZE = 128

NSA_HEADS = 8
NSA_GROUPS = 2
NSA_HPG = NSA_HEADS // NSA_GROUPS
NSA_DH = 64
NSA_CMP_LEN = 32
NSA_CMP_STRIDE = 16
NSA_CMP_R = NSA_CMP_LEN // NSA_CMP_STRIDE
NSA_CMP_HID = 2 * NSA_DH
NSA_SEL_BLOCK = 64
NSA_TOPN = 16
NSA_WINDOW = 512
MLA_HEADS = 4
MLA_Q_LORA = 256
MLA_KV_LORA = 128
MLA_NOPE = 64
MLA_ROPE = 32
MLA_V = 64
MLA_SCALE = (MLA_NOPE + MLA_ROPE) ** -0.5
ROPE_THETA = 10000.0
MOBA_HEADS = 4
MOBA_DH = 64
MOBA_BLOCK = 256
MOBA_TOPK = 3
FFN_HIDDEN = ((8 * D_MODEL + 3 * 256 - 1) // (3 * 256)) * 256
N_BRANCH = 3
N_MOD = 6
Q_BLOCK = 128
N_ALIBI = NSA_HEADS + MOBA_HEADS
RMS_EPS = 1e-6
NSA_OUT = NSA_HEADS * NSA_DH
MLA_OUT = MLA_HEADS * MLA_V
MOBA_OUT = MOBA_HEADS * MOBA_DH
IN_WIDTHS = (NSA_HEADS * NSA_DH, N_BRANCH * 2 * NSA_GROUPS * NSA_DH, N_BRANCH * NSA_HEADS,
             MLA_Q_LORA, MLA_KV_LORA, MLA_ROPE, 3 * MOBA_HEADS * MOBA_DH, N_BRANCH * D_MODEL)
N_IN = sum(IN_WIDTHS)

kernel_name = 'hybrid_nsa_mla_moba_decoder_step'


def rmsnorm(x, g):
    xf = x.astype(jnp.float32)
    xf = xf * lax.rsqrt(jnp.mean(xf * xf, axis=-1, keepdims=True) + RMS_EPS)
    return (xf * g.astype(jnp.float32)).astype(x.dtype)


def masked_softmax(s, mask):
    s = jnp.where(mask, s.astype(jnp.float32), -jnp.inf)
    m = jnp.max(s, axis=-1, keepdims=True)
    m = jnp.where(jnp.isfinite(m), m, 0.0)
    e = jnp.where(mask, jnp.exp(s - m), 0.0)
    return e / jnp.maximum(jnp.sum(e, axis=-1, keepdims=True), 1e-30)


def softmax_parts(scores, masks):
    s = jnp.concatenate([a.astype(jnp.float32) for a in scores], axis=-1)
    m = jnp.concatenate([jnp.broadcast_to(b, a.shape) for a, b in zip(scores, masks)], axis=-1)
    cuts = np.cumsum([a.shape[-1] for a in scores])[:-1].tolist()
    return jnp.split(masked_softmax(s, m), cuts, axis=-1)


def alibi_slopes():
    s = 2.0 ** (-8.0 * np.arange(1, N_ALIBI + 1) / N_ALIBI)
    step = N_ALIBI // MOBA_HEADS
    moba_idx = np.arange(MOBA_HEADS) * step + step - 1
    nsa_idx = np.setdiff1d(np.arange(N_ALIBI), moba_idx)
    return (jnp.asarray(s[nsa_idx], jnp.float32).reshape(NSA_GROUPS, NSA_HPG),
            jnp.asarray(s[moba_idx], jnp.float32))


def rope(x, pos):
    half = x.shape[-1] // 2
    inv = ROPE_THETA ** (-jnp.arange(half, dtype=jnp.float32) / half)
    ang = pos.astype(jnp.float32)[..., None] * inv
    cos, sin = jnp.cos(ang), jnp.sin(ang)
    x1, x2 = x[..., :half].astype(jnp.float32), x[..., half:].astype(jnp.float32)
    return jnp.concatenate([x1 * cos - x2 * sin, x1 * sin + x2 * cos], axis=-1).astype(x.dtype)


def nsa_compress(rows, w1, pe, w2):
    B, T = rows.shape[:2]
    nc = (T - NSA_CMP_LEN) // NSA_CMP_STRIDE + 1
    nch = T // NSA_CMP_STRIDE
    chunks = rows[:, :nch * NSA_CMP_STRIDE].reshape(B, nch, NSA_CMP_STRIDE, NSA_GROUPS, NSA_DH)
    hid = jnp.einsum('rcd,rcdh->h', pe.reshape(NSA_CMP_R, NSA_CMP_STRIDE, NSA_DH), w1)
    for r in range(NSA_CMP_R):
        hid = hid + jnp.einsum('bncgd,cdh->bngh', chunks[:, r:r + nc], w1[r])
    return jnp.einsum('bngh,hd->bngd', jax.nn.silu(hid), w2)


def nsa_block_cover(nc, ns):
    c0 = jnp.arange(nc) * NSA_CMP_STRIDE
    s0 = jnp.arange(ns) * NSA_SEL_BLOCK
    return ((c0[:, None] < s0[None, :] + NSA_SEL_BLOCK) & (c0[:, None] + NSA_CMP_LEN > s0[None, :])).astype(jnp.float32)


def nsa_attend(q, gate, t, kc, vc, c_end, cover, n_top, gather_sel, k_own, v_own, own_pos,
               k_win, v_win, win_pos, slopes):
    scale = NSA_DH ** -0.5
    sl = slopes[:, :, None]
    d_c = t[:, None] - c_end[None, :]
    s_c = jnp.einsum('bqghd,bngd->bqghn', q, kc) * scale - sl * d_c.astype(jnp.float32)[:, None, None, :]
    p_c = masked_softmax(s_c, (d_c >= 0)[:, None, None, :])
    o_c = jnp.einsum('bqghn,bngd->bqghd', p_c.astype(vc.dtype), vc)
    ns = cover.shape[1]
    imp = jnp.einsum('bqghn,nj->bqgj', p_c, cover)
    cand = (jnp.arange(ns)[None, :] < (t // NSA_SEL_BLOCK)[:, None])[None, :, None, :]
    top_v, top_i = lax.top_k(jnp.where(cand, imp, -jnp.inf), n_top)
    k_sel, v_sel = gather_sel(top_i)
    sel_pos = top_i[..., None] * NSA_SEL_BLOCK + jnp.arange(NSA_SEL_BLOCK)
    d_s = (t[None, :, None, None, None] - sel_pos).astype(jnp.float32)[:, :, :, None]
    s_s = jnp.einsum('bqghd,bqgkjd->bqghkj', q, k_sel) * scale - slopes[:, :, None, None] * d_s
    m_s = jnp.broadcast_to(jnp.isfinite(top_v)[:, :, :, None, :, None], s_s.shape)
    d_o = t[:, None] - own_pos[None, :]
    s_o = jnp.einsum('bqghd,bsgd->bqghs', q, k_own) * scale - sl * d_o.astype(jnp.float32)[:, None, None, :]
    m_o = ((d_o >= 0) & ((own_pos[None, :] // NSA_SEL_BLOCK) == (t[:, None] // NSA_SEL_BLOCK)))[:, None, None, :]
    lead = s_s.shape[:4]
    p_o, p_s = softmax_parts([s_o, s_s.reshape(lead + (-1,))], [m_o, m_s.reshape(lead + (-1,))])
    o_s = (jnp.einsum('bqghs,bsgd->bqghd', p_o.astype(v_own.dtype), v_own)
           + jnp.einsum('bqghkj,bqgkjd->bqghd', p_s.reshape(s_s.shape).astype(v_sel.dtype), v_sel))
    d_w = t[:, None] - win_pos[None, :]
    s_w = jnp.einsum('bqghd,bsgd->bqghs', q, k_win) * scale - sl * d_w.astype(jnp.float32)[:, None, None, :]
    m_w = ((d_w >= 0) & (d_w < NSA_WINDOW) & (win_pos[None, :] >= 0))[:, None, None, :]
    p_w = masked_softmax(s_w, m_w)
    o_w = jnp.einsum('bqghs,bsgd->bqghd', p_w.astype(v_win.dtype), v_win)
    return gate[..., 0:1] * o_c + gate[..., 1:2] * o_s + gate[..., 2:3] * o_w


def nsa_prompt(q, kv, gate, slopes, w1, pe, w2):
    B, S = q.shape[:2]
    kc = nsa_compress(kv[:, :, 0, 0], w1[0], pe[0], w2[0])
    vc = nsa_compress(kv[:, :, 0, 1], w1[1], pe[1], w2[1])
    nc = kc.shape[1]
    c_end = jnp.arange(nc) * NSA_CMP_STRIDE + NSA_CMP_LEN - 1
    ns = S // NSA_SEL_BLOCK
    cover = nsa_block_cover(nc, ns)
    n_top = min(NSA_TOPN, ns)
    k_slc, v_slc = kv[:, :, 1, 0], kv[:, :, 1, 1]
    to_blocks = lambda a: a.reshape(B, ns, NSA_SEL_BLOCK, NSA_GROUPS, NSA_DH).transpose(0, 3, 1, 2, 4)
    k_blk, v_blk = to_blocks(k_slc), to_blocks(v_slc)
    bidx = jnp.arange(B)[:, None, None, None]
    gidx = jnp.arange(NSA_GROUPS)[None, None, :, None]
    gather_sel = lambda ti: (k_blk[bidx, gidx, ti], v_blk[bidx, gidx, ti])
    pad = ((0, 0), (NSA_WINDOW, 0), (0, 0), (0, 0))
    k_win, v_win = jnp.pad(kv[:, :, 2, 0], pad), jnp.pad(kv[:, :, 2, 1], pad)

    def block(qi):
        q0 = qi * Q_BLOCK
        t = q0 + jnp.arange(Q_BLOCK)
        cut = lambda a, n: lax.dynamic_slice_in_dim(a, q0, n, axis=1)
        return nsa_attend(cut(q, Q_BLOCK), cut(gate, Q_BLOCK), t, kc, vc, c_end, cover, n_top, gather_sel,
                          cut(k_slc, Q_BLOCK), cut(v_slc, Q_BLOCK), t,
                          cut(k_win, NSA_WINDOW + Q_BLOCK), cut(v_win, NSA_WINDOW + Q_BLOCK),
                          q0 - NSA_WINDOW + jnp.arange(NSA_WINDOW + Q_BLOCK), slopes)

    o = lax.map(block, jnp.arange(S // Q_BLOCK))
    return jnp.moveaxis(o, 0, 1).reshape(B, S, NSA_OUT)


def nsa_decode(q, kv, gate, slopes, w1, pe, w2, l, cmp_pool, slc_k_pool, slc_v_pool, win_buf, page_table):
    DB, DS = q.shape[:2]
    t = PAST_LEN + jnp.arange(DS)
    past_cmp = cmp_pool[l, page_table].reshape(DB, PAST_LEN, 2, NSA_GROUPS, NSA_DH)
    kc = nsa_compress(jnp.concatenate([past_cmp[:, :, 0], kv[:, :, 0, 0]], axis=1), w1[0], pe[0], w2[0])
    vc = nsa_compress(jnp.concatenate([past_cmp[:, :, 1], kv[:, :, 0, 1]], axis=1), w1[1], pe[1], w2[1])
    nc = kc.shape[1]
    c_end = jnp.arange(nc) * NSA_CMP_STRIDE + NSA_CMP_LEN - 1
    ns = PAST_LEN // NSA_SEL_BLOCK
    cover = nsa_block_cover(nc, ns)
    n_top = min(NSA_TOPN, ns)
    bidx = jnp.arange(DB)[:, None, None, None, None]
    gidx = jnp.arange(NSA_GROUPS)[None, None, :, None, None]

    def gather_sel(ti):
        pos = ti[..., None] * NSA_SEL_BLOCK + jnp.arange(NSA_SEL_BLOCK)
        phys = page_table[bidx, pos // PAGE_SIZE]
        off = pos % PAGE_SIZE
        return slc_k_pool[l, phys, off, gidx], slc_v_pool[l, phys, off, gidx]

    L = win_buf.shape[1]
    win_k = jnp.concatenate([win_buf[:, :, 0], kv[:, :, 2, 0]], axis=1)
    win_v = jnp.concatenate([win_buf[:, :, 1], kv[:, :, 2, 1]], axis=1)
    win_pos = PAST_LEN - L + jnp.arange(L + DS)
    o = nsa_attend(q, gate, t, kc, vc, c_end, cover, n_top, gather_sel, kv[:, :, 1, 0], kv[:, :, 1, 1], t,
                   win_k, win_v, win_pos, slopes)
    keep = min(NSA_WINDOW, PAST_LEN + DS)
    new_win = jnp.concatenate([win_buf, kv[:, :, 2]], axis=1)[:, -keep:]
    return o.reshape(DB, DS, NSA_OUT), new_win


def mla_attend(q_nope, q_pe, lat, kpe, mask, w_uk, w_uv):
    q_lat = jnp.einsum('bqhn,chn->bqhc', q_nope, w_uk)
    s = (jnp.einsum('bqhc,bkc->bqhk', q_lat, lat) + jnp.einsum('bqhr,bkr->bqhk', q_pe, kpe)) * MLA_SCALE
    p = masked_softmax(s, mask)
    ctx = jnp.einsum('bqhk,bkc->bqhc', p.astype(lat.dtype), lat)
    return jnp.einsum('bqhc,chv->bqhv', ctx, w_uv)


def mla_prompt(q_nope, q_pe, latent, kpe, w_uk, w_uv):
    B, S = q_nope.shape[:2]
    k_pos = jnp.arange(S)

    def block(qi):
        q0 = qi * Q_BLOCK
        t = q0 + jnp.arange(Q_BLOCK)
        qn = lax.dynamic_slice_in_dim(q_nope, q0, Q_BLOCK, axis=1)
        qp = lax.dynamic_slice_in_dim(q_pe, q0, Q_BLOCK, axis=1)
        return mla_attend(qn, qp, latent, kpe, (k_pos[None, :] <= t[:, None])[:, None, :], w_uk, w_uv)

    o = lax.map(block, jnp.arange(S // Q_BLOCK))
    return jnp.moveaxis(o, 0, 1).reshape(B, S, MLA_OUT)


def mla_decode(q_nope, q_pe, latent, kpe, w_uk, w_uv, l, lat_pool, kpe_pool, page_table):
    DB, DS = q_nope.shape[:2]
    lat = jnp.concatenate([lat_pool[l, page_table].reshape(DB, PAST_LEN, MLA_KV_LORA), latent], axis=1)
    kp = jnp.concatenate([kpe_pool[l, page_table].reshape(DB, PAST_LEN, MLA_ROPE), kpe], axis=1)
    t = PAST_LEN + jnp.arange(DS)
    k_pos = jnp.arange(PAST_LEN + DS)
    o = mla_attend(q_nope, q_pe, lat, kp, (k_pos[None, :] <= t[:, None])[:, None, :], w_uk, w_uv)
    return o.reshape(DB, DS, MLA_OUT)


def moba_attend(q, t, k_mean, n_sel, gather_sel, k_own, v_own, own_pos, slopes):
    scale = MOBA_DH ** -0.5
    sl = slopes[:, None]
    d_o = t[:, None] - own_pos[None, :]
    s_o = jnp.einsum('bqhd,bkhd->bqhk', q, k_own) * scale - sl * d_o.astype(jnp.float32)[:, None, :]
    m_o = (d_o >= 0)[:, None, :]
    if n_sel == 0:
        return jnp.einsum('bqhk,bkhd->bqhd', masked_softmax(s_o, m_o).astype(v_own.dtype), v_own)
    nf = k_mean.shape[1]
    gs = jnp.einsum('bqhd,bjhd->bqhj', q.astype(jnp.float32), k_mean)
    cand = (jnp.arange(nf)[None, :] < (t // MOBA_BLOCK)[:, None])[None, :, None, :]
    g_v, g_i = lax.top_k(jnp.where(cand, gs, -jnp.inf), n_sel)
    k_sel, v_sel = gather_sel(g_i)
    sel_pos = g_i[..., None] * MOBA_BLOCK + jnp.arange(MOBA_BLOCK)
    s_s = (jnp.einsum('bqhd,bqhkjd->bqhkj', q, k_sel) * scale
           - slopes[:, None, None] * (t[None, :, None, None, None] - sel_pos).astype(jnp.float32))
    m_s = jnp.broadcast_to(jnp.isfinite(g_v)[..., None], s_s.shape)
    lead = s_s.shape[:3]
    p_o, p_s = softmax_parts([s_o, s_s.reshape(lead + (-1,))], [m_o, m_s.reshape(lead + (-1,))])
    return (jnp.einsum('bqhk,bkhd->bqhd', p_o.astype(v_own.dtype), v_own)
            + jnp.einsum('bqhkj,bqhkjd->bqhd', p_s.reshape(s_s.shape).astype(v_sel.dtype), v_sel))


def moba_prompt(q, k, v, slopes):
    B, S = q.shape[:2]
    n_full = S // MOBA_BLOCK
    n_blk = -(-S // MOBA_BLOCK)
    pad = ((0, 0), (0, n_blk * MOBA_BLOCK - S), (0, 0), (0, 0))
    k_pad, v_pad = jnp.pad(k, pad), jnp.pad(v, pad)
    to_blocks = lambda a: a.reshape(B, n_blk, MOBA_BLOCK, MOBA_HEADS, MOBA_DH)
    n_sel = min(MOBA_TOPK, n_full)
    k_mean = jnp.mean(to_blocks(k_pad)[:, :n_full].astype(jnp.float32), axis=2) if n_sel > 0 else None
    k_blk = to_blocks(k_pad).transpose(0, 3, 1, 2, 4)
    v_blk = to_blocks(v_pad).transpose(0, 3, 1, 2, 4)
    bidx = jnp.arange(B)[:, None, None, None]
    hidx = jnp.arange(MOBA_HEADS)[None, None, :, None]
    gather_sel = lambda gi: (k_blk[bidx, hidx, gi], v_blk[bidx, hidx, gi])

    def block(qi):
        q0 = qi * Q_BLOCK
        t = q0 + jnp.arange(Q_BLOCK)
        own_start = (q0 // MOBA_BLOCK) * MOBA_BLOCK
        k_own = lax.dynamic_slice_in_dim(k_pad, own_start, MOBA_BLOCK, axis=1)
        v_own = lax.dynamic_slice_in_dim(v_pad, own_start, MOBA_BLOCK, axis=1)
        return moba_attend(lax.dynamic_slice_in_dim(q, q0, Q_BLOCK, axis=1), t, k_mean, n_sel, gather_sel,
                           k_own, v_own, own_start + jnp.arange(MOBA_BLOCK), slopes)

    o = lax.map(block, jnp.arange(S // Q_BLOCK))
    return jnp.moveaxis(o, 0, 1).reshape(B, S, MOBA_OUT)


def moba_decode(q, k, v, slopes, l, k_pool, v_pool, page_table):
    DB, DS = q.shape[:2]
    t = PAST_LEN + jnp.arange(DS)
    n_full = PAST_LEN // MOBA_BLOCK
    own_start = n_full * MOBA_BLOCK
    r0 = PAST_LEN - own_start
    n_sel = min(MOBA_TOPK, n_full)
    k_own, v_own = k, v
    if r0 > 0:
        pos0 = own_start + jnp.arange(r0)
        phys0 = page_table[:, pos0 // PAGE_SIZE]
        off0 = pos0 % PAGE_SIZE
        k_own = jnp.concatenate([k_pool[l, phys0, off0], k], axis=1)
        v_own = jnp.concatenate([v_pool[l, phys0, off0], v], axis=1)
    own_pos = own_start + jnp.arange(r0 + DS)
    k_mean = None
    if n_sel > 0:
        k_past = k_pool[l, page_table].reshape(DB, PAST_LEN, MOBA_HEADS, MOBA_DH)[:, :own_start]
        k_mean = jnp.mean(k_past.reshape(DB, n_full, MOBA_BLOCK, MOBA_HEADS, MOBA_DH).astype(jnp.float32), axis=2)
    bidx = jnp.arange(DB)[:, None, None, None, None]
    hidx = jnp.arange(MOBA_HEADS)[None, None, :, None, None]

    def gather_sel(gi):
        pos = gi[..., None] * MOBA_BLOCK + jnp.arange(MOBA_BLOCK)
        phys = page_table[bidx, pos // PAGE_SIZE]
        off = pos % PAGE_SIZE
        return k_pool[l, phys, off, hidx], v_pool[l, phys, off, hidx]

    o = moba_attend(q, t, k_mean, n_sel, gather_sel, k_own, v_own, own_pos, slopes)
    return o.reshape(DB, DS, MOBA_OUT)


def adaln(c, w, b):
    return (jax.nn.silu(c) @ w + b).reshape(c.shape[0], N_MOD, 1, D_MODEL)


def modulate(x, g, shift, scale):
    return rmsnorm(x, g) * (1 + scale) + shift


def swiglu(h, w_in_, w_out_):
    a, u = jnp.split(h @ w_in_, 2, axis=-1)
    return (jax.nn.silu(a) * u) @ w_out_


def mixer_inputs(h, pos, w_in, mla_q_norm, mla_w_uq, mla_kv_norm):
    lead = h.shape[:2]
    cuts = np.cumsum(IN_WIDTHS)[:-1].tolist()
    nq, nkv, ng, mql, mkvl, mkpe, mqkv, mg = jnp.split(h @ w_in, cuts, axis=-1)
    nsa_q = nq.reshape(lead + (NSA_GROUPS, NSA_HPG, NSA_DH))
    nsa_kv = nkv.reshape(lead + (N_BRANCH, 2, NSA_GROUPS, NSA_DH))
    nsa_g = jax.nn.sigmoid(ng.reshape(lead + (NSA_GROUPS, NSA_HPG, N_BRANCH)))
    mq = (rmsnorm(mql, mla_q_norm) @ mla_w_uq.reshape(MLA_Q_LORA, -1)).reshape(lead + (MLA_HEADS, MLA_NOPE + MLA_ROPE))
    q_nope = mq[..., :MLA_NOPE]
    q_pe = rope(mq[..., MLA_NOPE:], pos[:, None])
    latent = rmsnorm(mkvl, mla_kv_norm)
    kpe = rope(mkpe, pos)
    moba = mqkv.reshape(lead + (3, MOBA_HEADS, MOBA_DH))
    merge_g = jax.nn.sigmoid(mg.reshape(lead + (N_BRANCH, D_MODEL)))
    return (nsa_q, nsa_kv, nsa_g, q_nope, q_pe, latent, kpe, moba[:, :, 0], moba[:, :, 1], moba[:, :, 2], merge_g)


def mix_prompt(inp, nsa_sl, moba_sl, w1, pe, w2, w_uk, w_uv):
    nsa_q, nsa_kv, nsa_g, q_nope, q_pe, latent, kpe, moba_q, moba_k, moba_v, _ = inp
    o_a = nsa_prompt(nsa_q, nsa_kv, nsa_g, nsa_sl, w1, pe, w2)
    o_b = mla_prompt(q_nope, q_pe, latent, kpe, w_uk, w_uv)
    o_c = moba_prompt(moba_q, moba_k, moba_v, moba_sl)
    S = nsa_kv.shape[1]
    states = (nsa_kv[:, :, 0], nsa_kv[:, :, 1, 0], nsa_kv[:, :, 1, 1], nsa_kv[:, S - min(NSA_WINDOW, S):, 2],
              latent, kpe, moba_k, moba_v)
    return (o_a, o_b, o_c), states


def mix_sample(inp, nsa_sl, moba_sl, w1, pe, w2, w_uk, w_uv, l, cache_nsa_cmp_kv, cache_nsa_slc_k,
               cache_nsa_slc_v, state_nsa_win_kv, cache_mla_latent, cache_mla_kpe, cache_moba_k,
               cache_moba_v, page_table):
    nsa_q, nsa_kv, nsa_g, q_nope, q_pe, latent, kpe, moba_q, moba_k, moba_v, _ = inp
    o_a, new_win = nsa_decode(nsa_q, nsa_kv, nsa_g, nsa_sl, w1, pe, w2, l, cache_nsa_cmp_kv, cache_nsa_slc_k,
                              cache_nsa_slc_v, state_nsa_win_kv[l], page_table)
    o_b = mla_decode(q_nope, q_pe, latent, kpe, w_uk, w_uv, l, cache_mla_latent, cache_mla_kpe, page_table)
    o_c = moba_decode(moba_q, moba_k, moba_v, moba_sl, l, cache_moba_k, cache_moba_v, page_table)
    states = (nsa_kv[:, :, 0], nsa_kv[:, :, 1, 0], nsa_kv[:, :, 1, 1], new_win, latent, kpe, moba_k, moba_v)
    return (o_a, o_b, o_c), states


def trunk_layer(x, c, pos, mixers, ada_w, ada_b, norm_mix, norm_ffn, w_in, mla_q_norm, mla_w_uq, mla_kv_norm,
                w_br_nsa, w_br_mla, w_br_moba, w_out, ffn_w_in, ffn_w_out):
    mod = adaln(c, ada_w, ada_b)
    h = modulate(x, norm_mix, mod[:, 0], mod[:, 1])
    inp = mixer_inputs(h, pos, w_in, mla_q_norm, mla_w_uq, mla_kv_norm)
    (o_a, o_b, o_c), states = mixers(inp)
    g = inp[-1]
    mixed = (g[..., 0, :] * (o_a @ w_br_nsa) + g[..., 1, :] * (o_b @ w_br_mla)
             + g[..., 2, :] * (o_c @ w_br_moba)) @ w_out
    x = x + mod[:, 2] * mixed
    h = modulate(x, norm_ffn, mod[:, 3], mod[:, 4])
    x = x + mod[:, 5] * swiglu(h, ffn_w_in, ffn_w_out)
    return x, states


def setup_inputs(seed: int = 0) -> dict:
    key = jax.random.key(seed)
    keys = jax.random.split(key, 40)
    kit = iter([keys[i] for i in range(40)])
    nrm = lambda shape, s=1.0: jax.random.normal(next(kit), shape, jnp.float32) * s
    n_pages = PAST_LEN // PAGE_SIZE
    n_used = DEC_BATCH * n_pages
    n_pool = n_used + n_used // 4
    win = min(NSA_WINDOW, PAST_LEN)
    G, DH = NSA_GROUPS, NSA_DH
    D = D_MODEL
    return {
        'x_prompt': nrm((BATCH, SEQ, D)),
        'x_sample': nrm((DEC_BATCH, DEC_SEQ, D)),
        'c_prompt': nrm((BATCH, D)),
        'c_sample': nrm((DEC_BATCH, D)),
        'cache_nsa_cmp_kv': nrm((DEPTH, n_pool, PAGE_SIZE, 2, G, DH)),
        'cache_nsa_slc_k': nrm((DEPTH, n_pool, PAGE_SIZE, G, DH)),
        'cache_nsa_slc_v': nrm((DEPTH, n_pool, PAGE_SIZE, G, DH)),
        'state_nsa_win_kv': nrm((DEPTH, DEC_BATCH, win, 2, G, DH)),
        'cache_mla_latent': nrm((DEPTH, n_pool, PAGE_SIZE, MLA_KV_LORA)),
        'cache_mla_kpe': nrm((DEPTH, n_pool, PAGE_SIZE, MLA_ROPE)),
        'cache_moba_k': nrm((DEPTH, n_pool, PAGE_SIZE, MOBA_HEADS, MOBA_DH)),
        'cache_moba_v': nrm((DEPTH, n_pool, PAGE_SIZE, MOBA_HEADS, MOBA_DH)),
        'page_table': jax.random.permutation(next(kit), n_pool)[:n_used].reshape(DEC_BATCH, n_pages).astype(jnp.int32),
        'ada_w': nrm((DEPTH, D, N_MOD * D), 0.5 * D ** -0.5),
        'ada_b': nrm((DEPTH, N_MOD * D), 0.01),
        'norm_mix': 1.0 + nrm((DEPTH, D), 0.02),
        'norm_ffn': 1.0 + nrm((DEPTH, D), 0.02),
        'w_in': nrm((DEPTH, D, N_IN), D ** -0.5),
        'nsa_cmp_w1': nrm((DEPTH, 2, NSA_CMP_R, NSA_CMP_STRIDE, DH, NSA_CMP_HID), (NSA_CMP_LEN * DH) ** -0.5),
        'nsa_cmp_pe': nrm((DEPTH, 2, NSA_CMP_LEN, DH), 0.1),
        'nsa_cmp_w2': nrm((DEPTH, 2, NSA_CMP_HID, DH), NSA_CMP_HID ** -0.5),
        'mla_q_norm': 1.0 + nrm((DEPTH, MLA_Q_LORA), 0.02),
        'mla_w_uq': nrm((DEPTH, MLA_Q_LORA, MLA_HEADS, MLA_NOPE + MLA_ROPE), MLA_Q_LORA ** -0.5),
        'mla_kv_norm': 1.0 + nrm((DEPTH, MLA_KV_LORA), 0.02),
        'mla_w_uk': nrm((DEPTH, MLA_KV_LORA, MLA_HEADS, MLA_NOPE), MLA_KV_LORA ** -0.5),
        'mla_w_uv': nrm((DEPTH, MLA_KV_LORA, MLA_HEADS, MLA_V), MLA_KV_LORA ** -0.5),
        'w_br_nsa': nrm((DEPTH, NSA_OUT, D), NSA_OUT ** -0.5),
        'w_br_mla': nrm((DEPTH, MLA_OUT, D), MLA_OUT ** -0.5),
        'w_br_moba': nrm((DEPTH, MOBA_OUT, D), MOBA_OUT ** -0.5),
        'w_out': nrm((DEPTH, D, D), D ** -0.5),
        'ffn_w_in': nrm((DEPTH, D, 2 * FFN_HIDDEN), D ** -0.5),
        'ffn_w_out': nrm((DEPTH, FFN_HIDDEN, D), FFN_HIDDEN ** -0.5),
        'final_norm': 1.0 + nrm((D,), 0.02),
    }


def reference(x_prompt, x_sample, c_prompt, c_sample, cache_nsa_cmp_kv, cache_nsa_slc_k, cache_nsa_slc_v,
              state_nsa_win_kv, cache_mla_latent, cache_mla_kpe, cache_moba_k, cache_moba_v, page_table,
              ada_w, ada_b, norm_mix, norm_ffn, w_in, nsa_cmp_w1, nsa_cmp_pe, nsa_cmp_w2,
              mla_q_norm, mla_w_uq, mla_kv_norm, mla_w_uk, mla_w_uv, w_br_nsa, w_br_mla, w_br_moba,
              w_out, ffn_w_in, ffn_w_out, final_norm):
    nsa_sl, moba_sl = alibi_slopes()
    pos_p = jnp.arange(x_prompt.shape[1])
    pos_s = PAST_LEN + jnp.arange(x_sample.shape[1])
    xp, xs = x_prompt, x_sample
    st_p, st_s = [], []
    for l in range(DEPTH):
        shared = (ada_w[l], ada_b[l], norm_mix[l], norm_ffn[l], w_in[l], mla_q_norm[l], mla_w_uq[l],
                  mla_kv_norm[l], w_br_nsa[l], w_br_mla[l], w_br_moba[l], w_out[l], ffn_w_in[l], ffn_w_out[l])
        mix_w = (nsa_sl, moba_sl, nsa_cmp_w1[l], nsa_cmp_pe[l], nsa_cmp_w2[l], mla_w_uk[l], mla_w_uv[l])
        xp, sp = trunk_layer(xp, c_prompt, pos_p, lambda inp: mix_prompt(inp, *mix_w), *shared)
        xs, ss = trunk_layer(
            xs, c_sample, pos_s,
            lambda inp: mix_sample(inp, *mix_w, l, cache_nsa_cmp_kv, cache_nsa_slc_k, cache_nsa_slc_v,
                                   state_nsa_win_kv, cache_mla_latent, cache_mla_kpe, cache_moba_k,
                                   cache_moba_v, page_table),
            *shared)
        st_p.append(sp)
        st_s.append(ss)
    y_prompt = rmsnorm(xp, final_norm)
    y_sample = rmsnorm(xs, final_norm)
    (p_cmp_kv, p_slc_k, p_slc_v, p_win_kv, p_latent, p_kpe, p_moba_k, p_moba_v) = [jnp.stack(a, axis=0) for a in zip(*st_p)]
    (s_cmp_kv, s_slc_k, s_slc_v, s_win_kv, s_latent, s_kpe, s_moba_k, s_moba_v) = [jnp.stack(a, axis=0) for a in zip(*st_s)]
    return (y_prompt, y_sample, p_cmp_kv, p_slc_k, p_slc_v, p_win_kv, p_latent, p_kpe, p_moba_k, p_moba_v,
            s_cmp_kv, s_slc_k, s_slc_v, s_win_kv, s_latent, s_kpe, s_moba_k, s_moba_v)
```

```python
import functools

import jax
import jax.numpy as jnp
from jax import lax
import numpy as np
from jax.experimental import pallas as pl
from jax.experimental.pallas import tpu as pltpu

D_MODEL = 1024
PAGE_SIZE = 128
NSA_HEADS = 8
NSA_GROUPS = 2
NSA_HPG = NSA_HEADS // NSA_GROUPS
NSA_DH = 64
NSA_CMP_LEN = 32
NSA_CMP_STRIDE = 16
NSA_CMP_R = NSA_CMP_LEN // NSA_CMP_STRIDE
NSA_SEL_BLOCK = 64
NSA_TOPN = 16
NSA_WINDOW = 512
MLA_HEADS = 4
MLA_Q_LORA = 256
MLA_KV_LORA = 128
MLA_NOPE = 64
MLA_ROPE = 32
MLA_V = 64
MLA_SCALE = (MLA_NOPE + MLA_ROPE) ** -0.5
ROPE_THETA = 10000.0
MOBA_HEADS = 4
MOBA_DH = 64
MOBA_BLOCK = 256
MOBA_TOPK = 3
FFN_HIDDEN = ((8 * D_MODEL + 3 * 256 - 1) // (3 * 256)) * 256
N_BRANCH = 3
N_MOD = 6
Q_BLOCK = 128
N_ALIBI = NSA_HEADS + MOBA_HEADS
RMS_EPS = 1e-6
NSA_OUT = NSA_HEADS * NSA_DH
MLA_OUT = MLA_HEADS * MLA_V
MOBA_OUT = MOBA_HEADS * MOBA_DH
IN_WIDTHS = (NSA_HEADS * NSA_DH, N_BRANCH * 2 * NSA_GROUPS * NSA_DH, N_BRANCH * NSA_HEADS,
             MLA_Q_LORA, MLA_KV_LORA, MLA_ROPE, 3 * MOBA_HEADS * MOBA_DH, N_BRANCH * D_MODEL)

LANE = 128
MM_TM = 256
MM_TN = 512


def _mm_kernel(a_ref, b_ref, o_ref):
    o_ref[...] = jnp.dot(a_ref[...].astype(jnp.bfloat16), b_ref[...].astype(jnp.bfloat16),
                         preferred_element_type=jnp.float32)


def pmm(a, b):
    lead = a.shape[:-1]
    k = a.shape[-1]
    n = b.shape[-1]
    a2 = a.reshape(-1, k)
    m = a2.shape[0]
    mp = -(-m // MM_TM) * MM_TM
    np_ = -(-n // MM_TN) * MM_TN
    a2 = jnp.pad(a2, ((0, mp - m), (0, 0)))
    b2 = jnp.pad(b, ((0, 0), (0, np_ - n)))
    out = pl.pallas_call(
        _mm_kernel,
        grid=(mp // MM_TM, np_ // MM_TN),
        in_specs=[pl.BlockSpec((MM_TM, k), lambda i, j: (i, 0)),
                  pl.BlockSpec((k, MM_TN), lambda i, j: (0, j))],
        out_specs=pl.BlockSpec((MM_TM, MM_TN), lambda i, j: (i, j)),
        out_shape=jax.ShapeDtypeStruct((mp, np_), jnp.float32),
        compiler_params=pltpu.CompilerParams(dimension_semantics=("parallel", "parallel")),
    )(a2, b2)
    return out[:m, :n].reshape(lead + (n,))


def rmsnorm(x, g):
    xf = x.astype(jnp.float32)
    xf = xf * lax.rsqrt(jnp.mean(xf * xf, axis=-1, keepdims=True) + RMS_EPS)
    return (xf * g.astype(jnp.float32)).astype(x.dtype)


def masked_softmax(s, mask):
    s = jnp.where(mask, s.astype(jnp.float32), -jnp.inf)
    m = jnp.max(s, axis=-1, keepdims=True)
    m = jnp.where(jnp.isfinite(m), m, 0.0)
    e = jnp.where(mask, jnp.exp(s - m), 0.0)
    return e / jnp.maximum(jnp.sum(e, axis=-1, keepdims=True), 1e-30)


def softmax_parts(scores, masks):
    s = jnp.concatenate([a.astype(jnp.float32) for a in scores], axis=-1)
    m = jnp.concatenate([jnp.broadcast_to(b, a.shape) for a, b in zip(scores, masks)], axis=-1)
    cuts = np.cumsum([a.shape[-1] for a in scores])[:-1].tolist()
    return jnp.split(masked_softmax(s, m), cuts, axis=-1)


def alibi_slopes():
    s = 2.0 ** (-8.0 * np.arange(1, N_ALIBI + 1) / N_ALIBI)
    step = N_ALIBI // MOBA_HEADS
    moba_idx = np.arange(MOBA_HEADS) * step + step - 1
    nsa_idx = np.setdiff1d(np.arange(N_ALIBI), moba_idx)
    return (jnp.asarray(s[nsa_idx], jnp.float32).reshape(NSA_GROUPS, NSA_HPG),
            jnp.asarray(s[moba_idx], jnp.float32))


def rope(x, pos):
    half = x.shape[-1] // 2
    inv = ROPE_THETA ** (-jnp.arange(half, dtype=jnp.float32) / half)
    ang = pos.astype(jnp.float32)[..., None] * inv
    cos, sin = jnp.cos(ang), jnp.sin(ang)
    x1, x2 = x[..., :half].astype(jnp.float32), x[..., half:].astype(jnp.float32)
    return jnp.concatenate([x1 * cos - x2 * sin, x1 * sin + x2 * cos], axis=-1).astype(x.dtype)


def nsa_compress(rows, w1, pe, w2):
    B, T = rows.shape[:2]
    nc = (T - NSA_CMP_LEN) // NSA_CMP_STRIDE + 1
    nch = T // NSA_CMP_STRIDE
    chunks = rows[:, :nch * NSA_CMP_STRIDE].reshape(B, nch, NSA_CMP_STRIDE, NSA_GROUPS, NSA_DH)
    hid = jnp.einsum('rcd,rcdh->h', pe.reshape(NSA_CMP_R, NSA_CMP_STRIDE, NSA_DH), w1)
    for r in range(NSA_CMP_R):
        hid = hid + jnp.einsum('bncgd,cdh->bngh', chunks[:, r:r + nc], w1[r])
    return jnp.einsum('bngh,hd->bngd', jax.nn.silu(hid), w2)


def nsa_block_cover(nc, ns):
    c0 = jnp.arange(nc) * NSA_CMP_STRIDE
    s0 = jnp.arange(ns) * NSA_SEL_BLOCK
    return ((c0[:, None] < s0[None, :] + NSA_SEL_BLOCK) & (c0[:, None] + NSA_CMP_LEN > s0[None, :])).astype(jnp.float32)


def nsa_attend(q, gate, t, kc, vc, c_end, cover, n_top, gather_sel, k_own, v_own, own_pos,
               k_win, v_win, win_pos, slopes):
    scale = NSA_DH ** -0.5
    sl = slopes[:, :, None]
    d_c = t[:, None] - c_end[None, :]
    s_c = jnp.einsum('bqghd,bngd->bqghn', q, kc) * scale - sl * d_c.astype(jnp.float32)[:, None, None, :]
    p_c = masked_softmax(s_c, (d_c >= 0)[:, None, None, :])
    o_c = jnp.einsum('bqghn,bngd->bqghd', p_c.astype(vc.dtype), vc)
    ns = cover.shape[1]
    imp = jnp.einsum('bqghn,nj->bqgj', p_c, cover)
    cand = (jnp.arange(ns)[None, :] < (t // NSA_SEL_BLOCK)[:, None])[None, :, None, :]
    top_v, top_i = lax.top_k(jnp.where(cand, imp, -jnp.inf), n_top)
    k_sel, v_sel = gather_sel(top_i)
    sel_pos = top_i[..., None] * NSA_SEL_BLOCK + jnp.arange(NSA_SEL_BLOCK)
    d_s = (t[None, :, None, None, None] - sel_pos).astype(jnp.float32)[:, :, :, None]
    s_s = jnp.einsum('bqghd,bqgkjd->bqghkj', q, k_sel) * scale - slopes[:, :, None, None] * d_s
    m_s = jnp.broadcast_to(jnp.isfinite(top_v)[:, :, :, None, :, None], s_s.shape)
    d_o = t[:, None] - own_pos[None, :]
    s_o = jnp.einsum('bqghd,bsgd->bqghs', q, k_own) * scale - sl * d_o.astype(jnp.float32)[:, None, None, :]
    m_o = ((d_o >= 0) & ((own_pos[None, :] // NSA_SEL_BLOCK) == (t[:, None] // NSA_SEL_BLOCK)))[:, None, None, :]
    lead = s_s.shape[:4]
    p_o, p_s = softmax_parts([s_o, s_s.reshape(lead + (-1,))], [m_o, m_s.reshape(lead + (-1,))])
    o_s = (jnp.einsum('bqghs,bsgd->bqghd', p_o.astype(v_own.dtype), v_own)
           + jnp.einsum('bqghkj,bqgkjd->bqghd', p_s.reshape(s_s.shape).astype(v_sel.dtype), v_sel))
    d_w = t[:, None] - win_pos[None, :]
    s_w = jnp.einsum('bqghd,bsgd->bqghs', q, k_win) * scale - sl * d_w.astype(jnp.float32)[:, None, None, :]
    m_w = ((d_w >= 0) & (d_w < NSA_WINDOW) & (win_pos[None, :] >= 0))[:, None, None, :]
    p_w = masked_softmax(s_w, m_w)
    o_w = jnp.einsum('bqghs,bsgd->bqghd', p_w.astype(v_win.dtype), v_win)
    return gate[..., 0:1] * o_c + gate[..., 1:2] * o_s + gate[..., 2:3] * o_w


def nsa_prompt(q, kv, gate, slopes, w1, pe, w2):
    B, S = q.shape[:2]
    kc = nsa_compress(kv[:, :, 0, 0], w1[0], pe[0], w2[0])
    vc = nsa_compress(kv[:, :, 0, 1], w1[1], pe[1], w2[1])
    nc = kc.shape[1]
    c_end = jnp.arange(nc) * NSA_CMP_STRIDE + NSA_CMP_LEN - 1
    ns = S // NSA_SEL_BLOCK
    cover = nsa_block_cover(nc, ns)
    n_top = min(NSA_TOPN, ns)
    k_slc, v_slc = kv[:, :, 1, 0], kv[:, :, 1, 1]
    to_blocks = lambda a: a.reshape(B, ns, NSA_SEL_BLOCK, NSA_GROUPS, NSA_DH).transpose(0, 3, 1, 2, 4)
    k_blk, v_blk = to_blocks(k_slc), to_blocks(v_slc)
    bidx = jnp.arange(B)[:, None, None, None]
    gidx = jnp.arange(NSA_GROUPS)[None, None, :, None]
    gather_sel = lambda ti: (k_blk[bidx, gidx, ti], v_blk[bidx, gidx, ti])
    pad = ((0, 0), (NSA_WINDOW, 0), (0, 0), (0, 0))
    k_win, v_win = jnp.pad(kv[:, :, 2, 0], pad), jnp.pad(kv[:, :, 2, 1], pad)

    def block(qi):
        q0 = qi * Q_BLOCK
        t = q0 + jnp.arange(Q_BLOCK)
        cut = lambda a, n: lax.dynamic_slice_in_dim(a, q0, n, axis=1)
        return nsa_attend(cut(q, Q_BLOCK), cut(gate, Q_BLOCK), t, kc, vc, c_end, cover, n_top, gather_sel,
                          cut(k_slc, Q_BLOCK), cut(v_slc, Q_BLOCK), t,
                          cut(k_win, NSA_WINDOW + Q_BLOCK), cut(v_win, NSA_WINDOW + Q_BLOCK),
                          q0 - NSA_WINDOW + jnp.arange(NSA_WINDOW + Q_BLOCK), slopes)

    o = lax.map(block, jnp.arange(S // Q_BLOCK))
    return jnp.moveaxis(o, 0, 1).reshape(B, S, NSA_OUT)


def nsa_decode(q, kv, gate, slopes, w1, pe, w2, l, cmp_pool, slc_k_pool, slc_v_pool, win_buf, page_table, past_len):
    DB, DS = q.shape[:2]
    t = past_len + jnp.arange(DS)
    past_cmp = cmp_pool[l, page_table].reshape(DB, past_len, 2, NSA_GROUPS, NSA_DH)
    kc = nsa_compress(jnp.concatenate([past_cmp[:, :, 0], kv[:, :, 0, 0]], axis=1), w1[0], pe[0], w2[0])
    vc = nsa_compress(jnp.concatenate([past_cmp[:, :, 1], kv[:, :, 0, 1]], axis=1), w1[1], pe[1], w2[1])
    nc = kc.shape[1]
    c_end = jnp.arange(nc) * NSA_CMP_STRIDE + NSA_CMP_LEN - 1
    ns = past_len // NSA_SEL_BLOCK
    cover = nsa_block_cover(nc, ns)
    n_top = min(NSA_TOPN, ns)
    bidx = jnp.arange(DB)[:, None, None, None, None]
    gidx = jnp.arange(NSA_GROUPS)[None, None, :, None, None]

    def gather_sel(ti):
        pos = ti[..., None] * NSA_SEL_BLOCK + jnp.arange(NSA_SEL_BLOCK)
        phys = page_table[bidx, pos // PAGE_SIZE]
        off = pos % PAGE_SIZE
        return slc_k_pool[l, phys, off, gidx], slc_v_pool[l, phys, off, gidx]

    L = win_buf.shape[1]
    win_k = jnp.concatenate([win_buf[:, :, 0], kv[:, :, 2, 0]], axis=1)
    win_v = jnp.concatenate([win_buf[:, :, 1], kv[:, :, 2, 1]], axis=1)
    win_pos = past_len - L + jnp.arange(L + DS)
    o = nsa_attend(q, gate, t, kc, vc, c_end, cover, n_top, gather_sel, kv[:, :, 1, 0], kv[:, :, 1, 1], t,
                   win_k, win_v, win_pos, slopes)
    keep = min(NSA_WINDOW, past_len + DS)
    new_win = jnp.concatenate([win_buf, kv[:, :, 2]], axis=1)[:, -keep:]
    return o.reshape(DB, DS, NSA_OUT), new_win


def mla_attend(q_nope, q_pe, lat, kpe, mask, w_uk, w_uv):
    q_lat = jnp.einsum('bqhn,chn->bqhc', q_nope, w_uk)
    s = (jnp.einsum('bqhc,bkc->bqhk', q_lat, lat) + jnp.einsum('bqhr,bkr->bqhk', q_pe, kpe)) * MLA_SCALE
    p = masked_softmax(s, mask)
    ctx = jnp.einsum('bqhk,bkc->bqhc', p.astype(lat.dtype), lat)
    return jnp.einsum('bqhc,chv->bqhv', ctx, w_uv)


def mla_prompt(q_nope, q_pe, latent, kpe, w_uk, w_uv):
    B, S = q_nope.shape[:2]
    k_pos = jnp.arange(S)

    def block(qi):
        q0 = qi * Q_BLOCK
        t = q0 + jnp.arange(Q_BLOCK)
        qn = lax.dynamic_slice_in_dim(q_nope, q0, Q_BLOCK, axis=1)
        qp = lax.dynamic_slice_in_dim(q_pe, q0, Q_BLOCK, axis=1)
        return mla_attend(qn, qp, latent, kpe, (k_pos[None, :] <= t[:, None])[:, None, :], w_uk, w_uv)

    o = lax.map(block, jnp.arange(S // Q_BLOCK))
    return jnp.moveaxis(o, 0, 1).reshape(B, S, MLA_OUT)


def mla_decode(q_nope, q_pe, latent, kpe, w_uk, w_uv, l, lat_pool, kpe_pool, page_table, past_len):
    DB, DS = q_nope.shape[:2]
    lat = jnp.concatenate([lat_pool[l, page_table].reshape(DB, past_len, MLA_KV_LORA), latent], axis=1)
    kp = jnp.concatenate([kpe_pool[l, page_table].reshape(DB, past_len, MLA_ROPE), kpe], axis=1)
    t = past_len + jnp.arange(DS)
    k_pos = jnp.arange(past_len + DS)
    o = mla_attend(q_nope, q_pe, lat, kp, (k_pos[None, :] <= t[:, None])[:, None, :], w_uk, w_uv)
    return o.reshape(DB, DS, MLA_OUT)


def moba_attend(q, t, k_mean, n_sel, gather_sel, k_own, v_own, own_pos, slopes):
    scale = MOBA_DH ** -0.5
    sl = slopes[:, None]
    d_o = t[:, None] - own_pos[None, :]
    s_o = jnp.einsum('bqhd,bkhd->bqhk', q, k_own) * scale - sl * d_o.astype(jnp.float32)[:, None, :]
    m_o = (d_o >= 0)[:, None, :]
    if n_sel == 0:
        return jnp.einsum('bqhk,bkhd->bqhd', masked_softmax(s_o, m_o).astype(v_own.dtype), v_own)
    nf = k_mean.shape[1]
    gs = jnp.einsum('bqhd,bjhd->bqhj', q.astype(jnp.float32), k_mean)
    cand = (jnp.arange(nf)[None, :] < (t // MOBA_BLOCK)[:, None])[None, :, None, :]
    g_v, g_i = lax.top_k(jnp.where(cand, gs, -jnp.inf), n_sel)
    k_sel, v_sel = gather_sel(g_i)
    sel_pos = g_i[..., None] * MOBA_BLOCK + jnp.arange(MOBA_BLOCK)
    s_s = (jnp.einsum('bqhd,bqhkjd->bqhkj', q, k_sel) * scale
           - slopes[:, None, None] * (t[None, :, None, None, None] - sel_pos).astype(jnp.float32))
    m_s = jnp.broadcast_to(jnp.isfinite(g_v)[..., None], s_s.shape)
    lead = s_s.shape[:3]
    p_o, p_s = softmax_parts([s_o, s_s.reshape(lead + (-1,))], [m_o, m_s.reshape(lead + (-1,))])
    return (jnp.einsum('bqhk,bkhd->bqhd', p_o.astype(v_own.dtype), v_own)
            + jnp.einsum('bqhkj,bqhkjd->bqhd', p_s.reshape(s_s.shape).astype(v_sel.dtype), v_sel))


def moba_prompt(q, k, v, slopes):
    B, S = q.shape[:2]
    n_full = S // MOBA_BLOCK
    n_blk = -(-S // MOBA_BLOCK)
    pad = ((0, 0), (0, n_blk * MOBA_BLOCK - S), (0, 0), (0, 0))
    k_pad, v_pad = jnp.pad(k, pad), jnp.pad(v, pad)
    to_blocks = lambda a: a.reshape(B, n_blk, MOBA_BLOCK, MOBA_HEADS, MOBA_DH)
    n_sel = min(MOBA_TOPK, n_full)
    k_mean = jnp.mean(to_blocks(k_pad)[:, :n_full].astype(jnp.float32), axis=2) if n_sel > 0 else None
    k_blk = to_blocks(k_pad).transpose(0, 3, 1, 2, 4)
    v_blk = to_blocks(v_pad).transpose(0, 3, 1, 2, 4)
    bidx = jnp.arange(B)[:, None, None, None]
    hidx = jnp.arange(MOBA_HEADS)[None, None, :, None]
    gather_sel = lambda gi: (k_blk[bidx, hidx, gi], v_blk[bidx, hidx, gi])

    def block(qi):
        q0 = qi * Q_BLOCK
        t = q0 + jnp.arange(Q_BLOCK)
        own_start = (q0 // MOBA_BLOCK) * MOBA_BLOCK
        k_own = lax.dynamic_slice_in_dim(k_pad, own_start, MOBA_BLOCK, axis=1)
        v_own = lax.dynamic_slice_in_dim(v_pad, own_start, MOBA_BLOCK, axis=1)
        return moba_attend(lax.dynamic_slice_in_dim(q, q0, Q_BLOCK, axis=1), t, k_mean, n_sel, gather_sel,
                           k_own, v_own, own_start + jnp.arange(MOBA_BLOCK), slopes)

    o = lax.map(block, jnp.arange(S // Q_BLOCK))
    return jnp.moveaxis(o, 0, 1).reshape(B, S, MOBA_OUT)


def moba_decode(q, k, v, slopes, l, k_pool, v_pool, page_table, past_len):
    DB, DS = q.shape[:2]
    t = past_len + jnp.arange(DS)
    n_full = past_len // MOBA_BLOCK
    own_start = n_full * MOBA_BLOCK
    r0 = past_len - own_start
    n_sel = min(MOBA_TOPK, n_full)
    k_own, v_own = k, v
    if r0 > 0:
        pos0 = own_start + jnp.arange(r0)
        phys0 = page_table[:, pos0 // PAGE_SIZE]
        off0 = pos0 % PAGE_SIZE
        k_own = jnp.concatenate([k_pool[l, phys0, off0], k], axis=1)
        v_own = jnp.concatenate([v_pool[l, phys0, off0], v], axis=1)
    own_pos = own_start + jnp.arange(r0 + DS)
    k_mean = None
    if n_sel > 0:
        k_past = k_pool[l, page_table].reshape(DB, past_len, MOBA_HEADS, MOBA_DH)[:, :own_start]
        k_mean = jnp.mean(k_past.reshape(DB, n_full, MOBA_BLOCK, MOBA_HEADS, MOBA_DH).astype(jnp.float32), axis=2)
    bidx = jnp.arange(DB)[:, None, None, None, None]
    hidx = jnp.arange(MOBA_HEADS)[None, None, :, None, None]

    def gather_sel(gi):
        pos = gi[..., None] * MOBA_BLOCK + jnp.arange(MOBA_BLOCK)
        phys = page_table[bidx, pos // PAGE_SIZE]
        off = pos % PAGE_SIZE
        return k_pool[l, phys, off, hidx], v_pool[l, phys, off, hidx]

    o = moba_attend(q, t, k_mean, n_sel, gather_sel, k_own, v_own, own_pos, slopes)
    return o.reshape(DB, DS, MOBA_OUT)


def adaln(c, w, b):
    return (pmm(jax.nn.silu(c), w) + b).reshape(c.shape[0], N_MOD, 1, D_MODEL)


def modulate(x, g, shift, scale):
    return rmsnorm(x, g) * (1 + scale) + shift


def swiglu(h, w_in_, w_out_):
    a, u = jnp.split(pmm(h, w_in_), 2, axis=-1)
    return pmm(jax.nn.silu(a) * u, w_out_)


def mixer_inputs(h, pos, w_in, mla_q_norm, mla_w_uq, mla_kv_norm):
    lead = h.shape[:2]
    cuts = np.cumsum(IN_WIDTHS)[:-1].tolist()
    nq, nkv, ng, mql, mkvl, mkpe, mqkv, mg = jnp.split(pmm(h, w_in), cuts, axis=-1)
    nsa_q = nq.reshape(lead + (NSA_GROUPS, NSA_HPG, NSA_DH))
    nsa_kv = nkv.reshape(lead + (N_BRANCH, 2, NSA_GROUPS, NSA_DH))
    nsa_g = jax.nn.sigmoid(ng.reshape(lead + (NSA_GROUPS, NSA_HPG, N_BRANCH)))
    mq = pmm(rmsnorm(mql, mla_q_norm), mla_w_uq.reshape(MLA_Q_LORA, -1)).reshape(lead + (MLA_HEADS, MLA_NOPE + MLA_ROPE))
    q_nope = mq[..., :MLA_NOPE]
    q_pe = rope(mq[..., MLA_NOPE:], pos[:, None])
    latent = rmsnorm(mkvl, mla_kv_norm)
    kpe = rope(mkpe, pos)
    moba = mqkv.reshape(lead + (3, MOBA_HEADS, MOBA_DH))
    merge_g = jax.nn.sigmoid(mg.reshape(lead + (N_BRANCH, D_MODEL)))
    return (nsa_q, nsa_kv, nsa_g, q_nope, q_pe, latent, kpe, moba[:, :, 0], moba[:, :, 1], moba[:, :, 2], merge_g)


def mix_prompt(inp, nsa_sl, moba_sl, w1, pe, w2, w_uk, w_uv):
    nsa_q, nsa_kv, nsa_g, q_nope, q_pe, latent, kpe, moba_q, moba_k, moba_v, _ = inp
    o_a = nsa_prompt(nsa_q, nsa_kv, nsa_g, nsa_sl, w1, pe, w2)
    o_b = mla_prompt(q_nope, q_pe, latent, kpe, w_uk, w_uv)
    o_c = moba_prompt(moba_q, moba_k, moba_v, moba_sl)
    S = nsa_kv.shape[1]
    states = (nsa_kv[:, :, 0], nsa_kv[:, :, 1, 0], nsa_kv[:, :, 1, 1], nsa_kv[:, S - min(NSA_WINDOW, S):, 2],
              latent, kpe, moba_k, moba_v)
    return (o_a, o_b, o_c), states


def mix_sample(inp, nsa_sl, moba_sl, w1, pe, w2, w_uk, w_uv, l, cache_nsa_cmp_kv, cache_nsa_slc_k,
               cache_nsa_slc_v, state_nsa_win_kv, cache_mla_latent, cache_mla_kpe, cache_moba_k,
               cache_moba_v, page_table, past_len):
    nsa_q, nsa_kv, nsa_g, q_nope, q_pe, latent, kpe, moba_q, moba_k, moba_v, _ = inp
    o_a, new_win = nsa_decode(nsa_q, nsa_kv, nsa_g, nsa_sl, w1, pe, w2, l, cache_nsa_cmp_kv, cache_nsa_slc_k,
                              cache_nsa_slc_v, state_nsa_win_kv[l], page_table, past_len)
    o_b = mla_decode(q_nope, q_pe, latent, kpe, w_uk, w_uv, l, cache_mla_latent, cache_mla_kpe, page_table, past_len)
    o_c = moba_decode(moba_q, moba_k, moba_v, moba_sl, l, cache_moba_k, cache_moba_v, page_table, past_len)
    states = (nsa_kv[:, :, 0], nsa_kv[:, :, 1, 0], nsa_kv[:, :, 1, 1], new_win, latent, kpe, moba_k, moba_v)
    return (o_a, o_b, o_c), states


def trunk_layer(x, c, pos, mixers, ada_w, ada_b, norm_mix, norm_ffn, w_in, mla_q_norm, mla_w_uq, mla_kv_norm,
                w_br_nsa, w_br_mla, w_br_moba, w_out, ffn_w_in, ffn_w_out):
    mod = adaln(c, ada_w, ada_b)
    h = modulate(x, norm_mix, mod[:, 0], mod[:, 1])
    inp = mixer_inputs(h, pos, w_in, mla_q_norm, mla_w_uq, mla_kv_norm)
    (o_a, o_b, o_c), states = mixers(inp)
    g = inp[-1]
    mixed = pmm(g[..., 0, :] * pmm(o_a, w_br_nsa) + g[..., 1, :] * pmm(o_b, w_br_mla)
                + g[..., 2, :] * pmm(o_c, w_br_moba), w_out)
    x = x + mod[:, 2] * mixed
    h = modulate(x, norm_ffn, mod[:, 3], mod[:, 4])
    x = x + mod[:, 5] * swiglu(h, ffn_w_in, ffn_w_out)
    return x, states


def kernel(x_prompt, x_sample, c_prompt, c_sample, cache_nsa_cmp_kv, cache_nsa_slc_k, cache_nsa_slc_v, state_nsa_win_kv, cache_mla_latent, cache_mla_kpe, cache_moba_k, cache_moba_v, page_table, ada_w, ada_b, norm_mix, norm_ffn, w_in, nsa_cmp_w1, nsa_cmp_pe, nsa_cmp_w2, mla_q_norm, mla_w_uq, mla_kv_norm, mla_w_uk, mla_w_uv, w_br_nsa, w_br_mla, w_br_moba, w_out, ffn_w_in, ffn_w_out, final_norm):
    depth = w_in.shape[0]
    past_len = page_table.shape[1] * PAGE_SIZE
    nsa_sl, moba_sl = alibi_slopes()
    pos_p = jnp.arange(x_prompt.shape[1])
    pos_s = past_len + jnp.arange(x_sample.shape[1])
    xp, xs = x_prompt, x_sample
    st_p, st_s = [], []
    for l in range(depth):
        shared = (ada_w[l], ada_b[l], norm_mix[l], norm_ffn[l], w_in[l], mla_q_norm[l], mla_w_uq[l],
                  mla_kv_norm[l], w_br_nsa[l], w_br_mla[l], w_br_moba[l], w_out[l], ffn_w_in[l], ffn_w_out[l])
        mix_w = (nsa_sl, moba_sl, nsa_cmp_w1[l], nsa_cmp_pe[l], nsa_cmp_w2[l], mla_w_uk[l], mla_w_uv[l])
        xp, sp = trunk_layer(xp, c_prompt, pos_p, lambda inp: mix_prompt(inp, *mix_w), *shared)
        xs, ss = trunk_layer(
            xs, c_sample, pos_s,
            lambda inp: mix_sample(inp, *mix_w, l, cache_nsa_cmp_kv, cache_nsa_slc_k, cache_nsa_slc_v,
                                   state_nsa_win_kv, cache_mla_latent, cache_mla_kpe, cache_moba_k,
                                   cache_moba_v, page_table, past_len),
            *shared)
        st_p.append(sp)
        st_s.append(ss)
    y_prompt = rmsnorm(xp, final_norm)
    y_sample = rmsnorm(xs, final_norm)
    outs_p = [jnp.stack(a, axis=0) for a in zip(*st_p)]
    outs_s = [jnp.stack(a, axis=0) for a in zip(*st_s)]
    return (y_prompt, y_sample, *outs_p, *outs_s)
```

```python
import functools

import jax
import jax.numpy as jnp
from jax import lax
import numpy as np
from jax.experimental import pallas as pl
from jax.experimental.pallas import tpu as pltpu
from jax.scipy.linalg import block_diag

D_MODEL = 1024
PAGE_SIZE = 128
NSA_HEADS = 8
NSA_GROUPS = 2
NSA_HPG = NSA_HEADS // NSA_GROUPS
NSA_DH = 64
NSA_CMP_LEN = 32
NSA_CMP_STRIDE = 16
NSA_CMP_R = NSA_CMP_LEN // NSA_CMP_STRIDE
NSA_CMP_HID = 2 * NSA_DH
NSA_SEL_BLOCK = 64
NSA_TOPN = 16
NSA_WINDOW = 512
MLA_HEADS = 4
MLA_Q_LORA = 256
MLA_KV_LORA = 128
MLA_NOPE = 64
MLA_ROPE = 32
MLA_V = 64
MLA_SCALE = (MLA_NOPE + MLA_ROPE) ** -0.5
ROPE_THETA = 10000.0
MOBA_HEADS = 4
MOBA_DH = 64
MOBA_BLOCK = 256
MOBA_TOPK = 3
FFN_HIDDEN = ((8 * D_MODEL + 3 * 256 - 1) // (3 * 256)) * 256
N_BRANCH = 3
N_MOD = 6
N_ALIBI = NSA_HEADS + MOBA_HEADS
RMS_EPS = 1e-6
IN_WIDTHS = (NSA_HEADS * NSA_DH, N_BRANCH * 2 * NSA_GROUPS * NSA_DH, N_BRANCH * NSA_HEADS,
             MLA_Q_LORA, MLA_KV_LORA, MLA_ROPE, 3 * MOBA_HEADS * MOBA_DH, N_BRANCH * D_MODEL)
NSA_SCALE = NSA_DH ** -0.5
MOBA_SCALE = MOBA_DH ** -0.5
_LOG2_DH = 6
_LOG2_SEL = 6
_LOG2_MOBA = 8
_LOG2_ROPE = 5
assert (1 << _LOG2_DH == NSA_DH == MOBA_DH and 1 << _LOG2_SEL == NSA_SEL_BLOCK
        and 1 << _LOG2_MOBA == MOBA_BLOCK and 1 << _LOG2_ROPE == MLA_ROPE)

F32 = jnp.float32
MXU_DTYPE = jnp.bfloat16
LANE = 128
SUBLANE = 8
VMEM_LIMIT = 56 * 1024 * 1024
TOKEN_TILE = 256
ATT_TQ = 128
ATT_TK = 256
DEC_Q = 8
DEC_NEW = 16
DEC_PAGES = 8
CMP_PAGES = 32
SEL_LANES = LANE
NEG_INF = float("-inf")


def _params(sem):
    return pltpu.CompilerParams(dimension_semantics=sem, vmem_limit_bytes=VMEM_LIMIT)


def _mx(x):
    return x.astype(MXU_DTYPE)


def _dot(a, b):
    return jnp.dot(_mx(a), _mx(b), preferred_element_type=F32)


_NT = (((1,), (1,)), ((), ()))


def _dot_nt(a, b):
    return lax.dot_general(_mx(a), _mx(b), _NT, preferred_element_type=F32)


def _split2(x):
    hi = _mx(x)
    lo = _mx(x - hi.astype(F32))
    return hi, lo


def _dot3_nt(a, b):
    ah, al = _split2(a)
    bh, bl = _split2(b)
    f = lambda x, y: lax.dot_general(x, y, _NT, preferred_element_type=F32)
    return f(ah, bh) + (f(ah, bl) + f(al, bh))


def _dot_exact01(x, e):
    h1 = _mx(x)
    r1 = x - h1.astype(F32)
    h2 = _mx(r1)
    h3 = _mx(r1 - h2.astype(F32))
    f = lambda a: jnp.dot(a, e, preferred_element_type=F32)
    return f(h1) + (f(h2) + f(h3))


def _sigmoid(x):
    return 1.0 / (1.0 + jnp.exp(-x))


def _silu(x):
    return x * _sigmoid(x)


def _rms(x, g):
    return x * lax.rsqrt(jnp.mean(x * x, axis=-1, keepdims=True) + RMS_EPS) * g


def _modulate(x, g, shift, scale):
    return _rms(x, g) * (1.0 + scale) + shift


def _iota(shape, dim):
    return lax.broadcasted_iota(jnp.int32, shape, dim)


def _topk_mask(vals, cand, k):
    lane = _iota(vals.shape, 1).astype(F32)
    cur = jnp.where(cand, vals, NEG_INF)
    sel = jnp.zeros(vals.shape, F32)
    for _ in range(k):
        mx = jnp.max(cur, axis=-1, keepdims=True)
        hit = (cur == mx) & (cur > NEG_INF)
        idx = jnp.min(jnp.where(hit, lane, 1e9), axis=-1, keepdims=True)
        one = lane == idx
        sel = jnp.where(one, 1.0, sel)
        cur = jnp.where(one, NEG_INF, cur)
    return sel


def alibi_slopes():
    s = 2.0 ** (-8.0 * np.arange(1, N_ALIBI + 1) / N_ALIBI)
    step = N_ALIBI // MOBA_HEADS
    moba_idx = np.arange(MOBA_HEADS) * step + step - 1
    nsa_idx = np.setdiff1d(np.arange(N_ALIBI), moba_idx)
    nsa = np.asarray(s[nsa_idx], np.float32).reshape(NSA_GROUPS, NSA_HPG)
    moba = np.asarray(s[moba_idx], np.float32)
    return [[float(v) for v in row] for row in nsa], [float(v) for v in moba]


def _adaln_kernel(c_ref, w_ref, b_ref, o_ref):
    o_ref[...] = _dot(_silu(c_ref[...]), w_ref[...]) + b_ref[...]


def adaln_all(c_all, ada_w, ada_b):
    depth, d, n = ada_w.shape
    r = c_all.shape[0]
    tn = 1024
    return pl.pallas_call(
        _adaln_kernel,
        grid=(depth, n // tn),
        in_specs=[pl.BlockSpec((r, d), lambda l, j: (0, 0)),
                  pl.BlockSpec((None, d, tn), lambda l, j: (l, 0, j)),
                  pl.BlockSpec((None, 1, tn), lambda l, j: (l, 0, j))],
        out_specs=pl.BlockSpec((None, r, tn), lambda l, j: (l, 0, j)),
        out_shape=jax.ShapeDtypeStruct((depth, r, n), F32),
        compiler_params=_params(("parallel", "parallel")),
        name="adaln",
    )(c_all, _mx(ada_w), ada_b.reshape(depth, 1, n))


_C_Q, _C_CMP, _C_SK, _C_SV, _C_WIN, _C_GATE, _C_MQL, _C_MKV, _C_F1, _C_F2, _C_MOBA, _C_END = (
    0, 512, 768, 896, 1024, 1280, 1408, 1664, 1792, 1920, 2048, 2816)


def _inproj_kernel(x_ref, sh_ref, sc_ref, g_ref, wa_ref, qn_ref, wuq_ref, wuk_ref, kvn_ref, c4_ref, s4_ref,
                   oq, ocmp, osk, osv, owin, ogate, oqmla, olatkpe, omq, omk, omv):
    h = _modulate(x_ref[...], g_ref[...], sh_ref[...], sc_ref[...])
    y = _dot(h, wa_ref[...])
    oq[...] = y[:, _C_Q:_C_CMP] * NSA_SCALE
    ocmp[...] = y[:, _C_CMP:_C_SK]
    osk[...] = y[:, _C_SK:_C_SV]
    osv[...] = y[:, _C_SV:_C_WIN]
    owin[...] = y[:, _C_WIN:_C_GATE]
    ogate[...] = _sigmoid(y[:, _C_GATE:_C_MQL])
    c4 = c4_ref[...]
    s4 = s4_ref[...]
    olatkpe[:, 0:LANE] = _rms(y[:, _C_MKV:_C_F1], kvn_ref[...])
    olatkpe[:, LANE:2 * LANE] = y[:, _C_F1:_C_F2] * c4 + y[:, _C_F2:_C_MOBA] * s4
    mq = _dot(_rms(y[:, _C_MQL:_C_MKV], qn_ref[...]), wuq_ref[...])
    qpe = (mq[:, 256:384] * c4 + mq[:, 384:512] * s4) * MLA_SCALE
    qlat = _dot(mq[:, 0:256], wuk_ref[...]) * MLA_SCALE
    head_of_lane = _iota(qpe.shape, 1) >> _LOG2_ROPE
    for hd in range(MLA_HEADS):
        oqmla[:, hd * 256:hd * 256 + LANE] = qlat[:, hd * LANE:(hd + 1) * LANE]
        oqmla[:, hd * 256 + LANE:(hd + 1) * 256] = jnp.where(head_of_lane == hd, qpe, 0.0)
    omq[...] = y[:, _C_MOBA:_C_MOBA + 256] * MOBA_SCALE
    omk[...] = y[:, _C_MOBA + 256:_C_MOBA + 512]
    omv[...] = y[:, _C_MOBA + 512:_C_END]


_INPROJ_WIDTHS = (512, 256, 128, 128, 256, 128, 1024, 256, 256, 256, 256)


def inproj(x, mod_ops, norm_g, lw, c4, s4, table_spec, tm):
    t, d = x.shape
    (sh, sh_spec), (sc, sc_spec) = mod_ops[0], mod_ops[1]
    full = lambda a: pl.BlockSpec(a.shape, lambda i: (0,) * a.ndim)
    row = lambda w: pl.BlockSpec((tm, w), lambda i: (i, 0))
    return pl.pallas_call(
        _inproj_kernel,
        grid=(t // tm,),
        in_specs=[row(d), sh_spec, sc_spec, full(norm_g), full(lw["w_a"]), full(lw["qn"]), full(lw["w_uq"]),
                  full(lw["w_uk"]), full(lw["kvn"]), table_spec, table_spec],
        out_specs=[row(w) for w in _INPROJ_WIDTHS],
        out_shape=[jax.ShapeDtypeStruct((t, w), F32) for w in _INPROJ_WIDTHS],
        compiler_params=_params(("parallel",)),
        name="inproj",
    )(x, sh, sc, norm_g, lw["w_a"], lw["qn"], lw["w_uq"], lw["w_uk"], lw["kvn"], c4, s4)


def _cmp_a_kernel(xa_ref, xb_ref, w_ref, o_ref, *, pp):
    cpp = PAGE_SIZE // NSA_CMP_STRIDE
    acc = None
    for c in range(NSA_CMP_STRIDE):
        rows = pl.ds(c, cpp, stride=NSA_CMP_STRIDE)
        xc = jnp.concatenate([xa_ref[:, rows, :].reshape(pp * cpp, LANE),
                              xb_ref[:, rows, :].reshape(pp * cpp, LANE)], axis=1)
        part = _dot(xc, w_ref[c])
        acc = part if acc is None else acc + part
    o_ref[...] = acc


def _cmp_a_t_kernel(xt_ref, w_ref, o_ref, xa_scr, xb_scr, *, pp):
    for p in range(pp):
        x = xt_ref[p].T
        xa_scr[p] = x[:, 0:LANE]
        xb_scr[p] = x[:, LANE:2 * LANE]
    _cmp_a_kernel(xa_scr, xb_scr, w_ref, o_ref, pp=pp)


def cmp_chunk_proj_t(pages_t, layer, w_chunks):
    _, npg, width, ps = pages_t.shape
    cpp = ps // NSA_CMP_STRIDE
    pp = _largest_divisor(npg, CMP_PAGES)
    nout = w_chunks.shape[-1]
    return pl.pallas_call(
        functools.partial(_cmp_a_t_kernel, pp=pp),
        grid=(npg // pp,),
        in_specs=[pl.BlockSpec((None, pp, width, ps), lambda i: (layer, i, 0, 0)),
                  pl.BlockSpec(w_chunks.shape, lambda i: (0, 0, 0))],
        out_specs=pl.BlockSpec((pp * cpp, nout), lambda i: (i, 0)),
        out_shape=jax.ShapeDtypeStruct((npg * cpp, nout), F32),
        scratch_shapes=[pltpu.VMEM((pp, ps, LANE), F32), pltpu.VMEM((pp, ps, LANE), F32)],
        compiler_params=_params(("parallel",)),
        name="cmp_chunk_proj_t",
    )(pages_t, w_chunks)


def _largest_divisor(n, cap):
    for p in range(min(cap, n), 0, -1):
        if n % p == 0:
            return p
    return 1


def cmp_chunk_proj(pages, layer, w_chunks):
    _, npg, ps, width = pages.shape
    cpp = ps // NSA_CMP_STRIDE
    pp = _largest_divisor(npg, CMP_PAGES)
    nout = w_chunks.shape[-1]
    return pl.pallas_call(
        functools.partial(_cmp_a_kernel, pp=pp),
        grid=(npg // pp,),
        in_specs=[pl.BlockSpec((None, pp, ps, LANE), lambda i: (layer, i, 0, 0)),
                  pl.BlockSpec((None, pp, ps, LANE), lambda i: (layer, i, 0, 1)),
                  pl.BlockSpec(w_chunks.shape, lambda i: (0, 0, 0))],
        out_specs=pl.BlockSpec((pp * cpp, nout), lambda i: (i, 0)),
        out_shape=jax.ShapeDtypeStruct((npg * cpp, nout), F32),
        compiler_params=_params(("parallel",)),
        name="cmp_chunk_proj",
    )(pages, pages, w_chunks)


def _cmp_asm_kernel(pt_ref, *refs, npages):
    a_refs = refs[:npages + 1]
    pe_ref, w2_ref, o_ref = refs[npages + 1:]
    rows = jnp.concatenate([r[...] for r in a_refs], axis=0)
    n = npages * SUBLANE
    half = rows.shape[1] // 2
    nxt = pltpu.roll(rows[:, half:], rows.shape[0] - 1, 0)
    hid = rows[0:n, 0:half] + nxt[0:n, :] + pe_ref[...]
    o_ref[...] = _dot(_silu(hid), w2_ref[...])


def cmp_assemble(a_pages, page_table, pe_const, w2_bd):
    nb, npg = page_table.shape
    p = _largest_divisor(npg, DEC_PAGES)
    cpp, wa = a_pages.shape[1:]
    a_spec = lambda k: pl.BlockSpec(
        (None, cpp, wa), lambda b, j, pt: (pt[b, jnp.minimum(j * p + k, npg - 1)], 0, 0))
    grid_spec = pltpu.PrefetchScalarGridSpec(
        num_scalar_prefetch=1,
        grid=(nb, npg // p),
        in_specs=[a_spec(k) for k in range(p + 1)]
        + [pl.BlockSpec(pe_const.shape, lambda b, j, pt: (0, 0)),
           pl.BlockSpec(w2_bd.shape, lambda b, j, pt: (0, 0))],
        out_specs=pl.BlockSpec((None, p * cpp, w2_bd.shape[1]), lambda b, j, pt: (b, j, 0)),
    )
    return pl.pallas_call(
        functools.partial(_cmp_asm_kernel, npages=p),
        grid_spec=grid_spec,
        out_shape=jax.ShapeDtypeStruct((nb, npg * cpp, w2_bd.shape[1]), F32),
        compiler_params=_params(("parallel", "parallel")),
        name="cmp_assemble",
    )(page_table, *([a_pages] * (p + 1)), pe_const, w2_bd)


def _nsa_cmp_kernel(q_ref, kcvc_ref, oc_ref, sel_ref, *, tq, t0, nc, ns, n_top, own_in_range, slopes):
    tbase = t0 + pl.program_id(1) * tq
    ncp = kcvc_ref.shape[0]
    kc = kcvc_ref[:, 0:LANE]
    vc = kcvc_ref[:, LANE:2 * LANE]
    r = NSA_HPG * tq
    lane = _iota((tq, LANE), 1)
    t_rows = tbase + (_iota((r, 1), 0) & (tq - 1))
    cidx = _iota((1, ncp), 1)
    d_c = t_rows - (cidx * NSA_CMP_STRIDE + (NSA_CMP_LEN - 1))
    valid = (d_c >= 0) & (cidx < nc)
    d_cf = d_c.astype(F32)
    ci = _iota((ncp, SEL_LANES), 0) * NSA_CMP_STRIDE
    sj = _iota((ncp, SEL_LANES), 1) * NSA_SEL_BLOCK
    cover = ((ci < sj + NSA_SEL_BLOCK) & (ci + NSA_CMP_LEN > sj)
             & (_iota((ncp, SEL_LANES), 0) < nc) & (_iota((ncp, SEL_LANES), 1) < ns))
    cover = _mx(cover.astype(F32))
    t_q = tbase + _iota((tq, 1), 0)
    blk_q = t_q >> _LOG2_SEL
    outs = []
    for g in range(NSA_GROUPS):
        in_half = (lane >= g * NSA_DH) & (lane < (g + 1) * NSA_DH)
        rows = jnp.concatenate(
            [jnp.where(in_half, q_ref[:, c * LANE:(c + 1) * LANE], 0.0) for c in range(NSA_HPG)], axis=0)
        slope_col = jnp.concatenate([jnp.full((tq, 1), s, F32) for s in slopes[g]], axis=0)
        s = _dot3_nt(rows, kc) - slope_col * d_cf
        s = jnp.where(valid, s, NEG_INF)
        m = jnp.max(s, axis=-1, keepdims=True)
        m = jnp.where(m > NEG_INF, m, 0.0)
        e = jnp.where(valid, jnp.exp(s - m), 0.0)
        p = e / jnp.maximum(jnp.sum(e, axis=-1, keepdims=True), 1e-30)
        outs.append(_dot(p, vc))
        psum = p[0:tq]
        for c in range(1, NSA_HPG):
            psum = psum + p[c * tq:(c + 1) * tq]
        ph, plo = _split2(psum)
        imp = jnp.dot(ph, cover, preferred_element_type=F32) + jnp.dot(plo, cover, preferred_element_type=F32)
        cand = (lane < blk_q) & (lane < ns)
        sel = _topk_mask(imp, cand, n_top)
        if own_in_range:
            sel = jnp.where(lane == blk_q, 1.0, sel)
        sel_ref[:, g * SEL_LANES:(g + 1) * SEL_LANES] = sel
    for c in range(NSA_HPG):
        oc_ref[:, c * LANE:(c + 1) * LANE] = jnp.where(
            lane < NSA_DH, outs[0][c * tq:(c + 1) * tq], outs[1][c * tq:(c + 1) * tq])


def nsa_cmp_select(q, kcvc, *, tq, t0, nc, ns, own_in_range, slopes):
    nb, sq, qw = q.shape
    ncp = kcvc.shape[1]
    kern = functools.partial(_nsa_cmp_kernel, tq=tq, t0=t0, nc=nc, ns=ns, n_top=min(NSA_TOPN, ns),
                             own_in_range=own_in_range, slopes=slopes)
    return pl.pallas_call(
        kern,
        grid=(nb, sq // tq),
        in_specs=[pl.BlockSpec((None, tq, qw), lambda b, i: (b, i, 0)),
                  pl.BlockSpec((None, ncp, kcvc.shape[2]), lambda b, i: (b, 0, 0))],
        out_specs=[pl.BlockSpec((None, tq, qw), lambda b, i: (b, i, 0)),
                   pl.BlockSpec((None, tq, NSA_GROUPS * SEL_LANES), lambda b, i: (b, i, 0))],
        out_shape=[jax.ShapeDtypeStruct((nb, sq, qw), F32),
                   jax.ShapeDtypeStruct((nb, sq, NSA_GROUPS * SEL_LANES), F32)],
        compiler_params=_params(("parallel", "parallel")),
        name="nsa_cmp_select",
    )(q, kcvc)


def _block_mean_kernel(k_ref, o_ref, *, nblk):
    o_ref[...] = jnp.zeros(o_ref.shape, F32)
    for j in range(nblk):
        blk = k_ref[j * MOBA_BLOCK:(j + 1) * MOBA_BLOCK, :]
        o_ref[j:j + 1, :] = jnp.sum(blk, axis=0, keepdims=True) * (1.0 / MOBA_BLOCK)


def moba_block_mean(k):
    nb, s, w = k.shape
    nblk = s // MOBA_BLOCK
    return pl.pallas_call(
        functools.partial(_block_mean_kernel, nblk=nblk),
        grid=(nb,),
        in_specs=[pl.BlockSpec((None, s, w), lambda b: (b, 0, 0))],
        out_specs=pl.BlockSpec((None, SEL_LANES, w), lambda b: (b, 0, 0)),
        out_shape=jax.ShapeDtypeStruct((nb, SEL_LANES, w), F32),
        compiler_params=_params(("parallel",)),
        name="moba_block_mean",
    )(k)


def _page_sum_kernel(x_ref, o_ref, *, pp):
    ones = jnp.ones((SUBLANE, x_ref.shape[2]), MXU_DTYPE)
    f = lambda a: lax.dot_general(ones, a, _NT, preferred_element_type=F32)
    for p in range(pp):
        x = x_ref[p]
        h1 = _mx(x)
        r1 = x - h1.astype(F32)
        h2 = _mx(r1)
        h3 = _mx(r1 - h2.astype(F32))
        o_ref[p:p + 1, :] = (f(h1) + (f(h2) + f(h3)))[0:1]


def page_sum(pages, layer):
    _, npg, w, ps = pages.shape
    pp = _largest_divisor(npg, CMP_PAGES)
    return pl.pallas_call(
        functools.partial(_page_sum_kernel, pp=pp),
        grid=(npg // pp,),
        in_specs=[pl.BlockSpec((None, pp, w, ps), lambda i: (layer, i, 0, 0))],
        out_specs=pl.BlockSpec((pp, w), lambda i: (i, 0)),
        out_shape=jax.ShapeDtypeStruct((npg, w), F32),
        compiler_params=_params(("parallel",)),
        name="page_sum",
    )(pages)


def _kmean_gather_kernel(pt_ref, tbl_ref, o_ref, *, nblk, ppb):
    b = pl.program_id(0)
    o_ref[...] = jnp.zeros(o_ref.shape, F32)
    for j in range(nblk):
        acc = tbl_ref[pl.ds(pt_ref[b, j * ppb], 1), :]
        for k in range(1, ppb):
            acc = acc + tbl_ref[pl.ds(pt_ref[b, j * ppb + k], 1), :]
        o_ref[j:j + 1, :] = acc * (1.0 / MOBA_BLOCK)


def moba_kmean_paged(page_sums, page_table):
    nb, npg = page_table.shape
    ppb = MOBA_BLOCK // PAGE_SIZE
    nblk = npg // ppb
    w = page_sums.shape[1]
    grid_spec = pltpu.PrefetchScalarGridSpec(
        num_scalar_prefetch=1,
        grid=(nb,),
        in_specs=[pl.BlockSpec(page_sums.shape, lambda b, pt: (0, 0))],
        out_specs=pl.BlockSpec((None, SEL_LANES, w), lambda b, pt: (b, 0, 0)),
    )
    return pl.pallas_call(
        functools.partial(_kmean_gather_kernel, nblk=nblk, ppb=ppb),
        grid_spec=grid_spec,
        out_shape=jax.ShapeDtypeStruct((nb, SEL_LANES, w), F32),
        compiler_params=_params(("arbitrary",)),
        name="moba_kmean_paged",
    )(page_table, page_sums)


def _moba_select_kernel(q_ref, km_ref, sel_ref, *, tq, t0, n_full, n_sel, own_in_range):
    tbase = t0 + pl.program_id(1) * tq
    q = q_ref[...]
    km = km_ref[...]
    lane_q = _iota(q.shape, 1)
    lane = _iota((tq, SEL_LANES), 1)
    blk_q = (tbase + _iota((tq, 1), 0)) >> _LOG2_MOBA
    cand = (lane < blk_q) & (lane < n_full)
    for hd in range(MOBA_HEADS):
        qh = jnp.where((lane_q >> _LOG2_DH) == hd, q, 0.0)
        gs = _dot3_nt(qh, km)
        sel = _topk_mask(gs, cand, n_sel)
        if own_in_range:
            sel = jnp.where(lane == blk_q, 1.0, sel)
        sel_ref[:, hd * SEL_LANES:(hd + 1) * SEL_LANES] = sel


def moba_select(q, kmean, *, tq, t0, n_full, own_in_range):
    nb, sq, qw = q.shape
    kern = functools.partial(_moba_select_kernel, tq=tq, t0=t0, n_full=n_full,
                             n_sel=min(MOBA_TOPK, n_full), own_in_range=own_in_range)
    return pl.pallas_call(
        kern,
        grid=(nb, sq // tq),
        in_specs=[pl.BlockSpec((None, tq, qw), lambda b, i: (b, i, 0)),
                  pl.BlockSpec((None,) + kmean.shape[1:], lambda b, i: (b, 0, 0))],
        out_specs=pl.BlockSpec((None, tq, MOBA_HEADS * SEL_LANES), lambda b, i: (b, i, 0)),
        out_shape=jax.ShapeDtypeStruct((nb, sq, MOBA_HEADS * SEL_LANES), F32),
        compiler_params=_params(("parallel", "parallel")),
        name="moba_select",
    )(q, kmean)


def _nsa_jobs(slopes):
    jobs = []
    for g in range(NSA_GROUPS):
        jobs.append(dict(qcols=[(c * LANE, LANE) for c in range(NSA_HPG)], qmask=(g * NSA_DH, (g + 1) * NSA_DH),
                         q2cols=None, kcol=(0, LANE), vcol=0, slopes=slopes[g],
                         sel=(g * SEL_LANES, _LOG2_SEL), outs=[(c * LANE, g * NSA_DH, (g + 1) * NSA_DH)
                                                       for c in range(NSA_HPG)]))
    return jobs


def _moba_jobs(slopes):
    jobs = []
    for hd in range(MOBA_HEADS):
        col, half = hd // 2, hd % 2
        jobs.append(dict(qcols=[(col * LANE, LANE)], qmask=(half * MOBA_DH, (half + 1) * MOBA_DH), q2cols=None,
                         kcol=(col * LANE, LANE), vcol=col * LANE, slopes=[slopes[hd]],
                         sel=(hd * SEL_LANES, _LOG2_MOBA), outs=[(col * LANE, half * MOBA_DH, (half + 1) * MOBA_DH)]))
    return jobs


def _stack_q(q_ref, job, tq):
    parts = []
    for off, w in job["qcols"]:
        x = q_ref[:, off:off + w]
        if job["qmask"] is not None:
            lo, hi = job["qmask"]
            ln = _iota(x.shape, 1)
            x = jnp.where((ln >= lo) & (ln < hi), x, 0.0)
        parts.append(x)
    qs = _mx(jnp.concatenate(parts, axis=0))
    q2 = None
    if job["q2cols"] is not None:
        q2 = _mx(jnp.concatenate([q_ref[:, off:off + w] for off, w in job["q2cols"]], axis=0))
    slope_col = None
    if job["slopes"] is not None:
        slope_col = jnp.concatenate([jnp.full((tq, 1), s, F32) for s in job["slopes"]], axis=0)
    return qs, q2, slope_col


def _softmax_step(s, allowed, v, m, l, acc, v_t=False):
    s = jnp.where(allowed, s, NEG_INF)
    m_new = jnp.maximum(m, jnp.max(s, axis=-1, keepdims=True))
    m_safe = jnp.where(m_new > NEG_INF, m_new, 0.0)
    alpha = jnp.exp(m - m_safe)
    p = jnp.where(allowed, jnp.exp(s - m_safe), 0.0)
    l = alpha * l + jnp.sum(p, axis=-1, keepdims=True)
    acc = alpha * acc + (_dot_nt(p, v) if v_t else _dot(p, v))
    return m_new, l, acc


def _write_outputs(o_ref, jobs, results, tq):
    lane = _iota((tq, LANE), 1)
    cols = {}
    for job, res in zip(jobs, results):
        for k, (off, lo, hi) in enumerate(job["outs"]):
            piece = res[k * tq:(k + 1) * tq]
            if (lo, hi) == (0, LANE):
                cols[off] = piece
            else:
                prev = cols.get(off, jnp.zeros((tq, LANE), F32))
                cols[off] = jnp.where((lane >= lo) & (lane < hi), piece, prev)
    for off, val in cols.items():
        o_ref[:, off:off + LANE] = val


def _flash_kernel(*refs, jobs, tq, tk, q_pos0, window, has_sel, has_v):
    refs = list(refs)
    q_ref = refs.pop(0)
    k_ref = refs.pop(0)
    v_ref = refs.pop(0) if has_v else k_ref
    sel_ref = refs.pop(0) if has_sel else None
    o_ref = refs.pop(0)
    t_lo = q_pos0 + pl.program_id(1) * tq
    n_kt = k_ref.shape[0] // tk
    kt_hi = jnp.minimum((t_lo + tq - 1) // tk + 1, n_kt)
    kt_lo = jnp.maximum(t_lo - window + 1, 0) // tk if window else 0
    results = []
    for job in jobs:
        qs, q2, slope_col = _stack_q(q_ref, job, tq)
        nq = len(job["qcols"])
        r = nq * tq
        t_rows = t_lo + (_iota((r, 1), 0) & (tq - 1))
        koff, kw = job["kcol"]
        voff = job["vcol"]
        selm = None
        if job["sel"] is not None and has_sel:
            soff, shift = job["sel"]
            selm = _mx(sel_ref[:, soff:soff + SEL_LANES])

        def body(kt, carry, qs=qs, slope_col=slope_col, t_rows=t_rows, koff=koff, kw=kw, voff=voff,
                 selm=selm, job=job, nq=nq, r=r):
            m, l, acc = carry
            k0 = pl.multiple_of(kt * tk, tk)
            ktile = k_ref[pl.ds(k0, tk), koff:koff + kw]
            vtile = v_ref[pl.ds(k0, tk), voff:voff + LANE]
            s = _dot_nt(qs, ktile)
            d = t_rows - (k0 + _iota((1, tk), 1))
            allowed = d >= 0
            if window:
                allowed = allowed & (d < window)
            if slope_col is not None:
                s = s - slope_col * d.astype(F32)
            if selm is not None:
                shift = job["sel"][1]
                expand = _mx((_iota((SEL_LANES, tk), 0) == ((k0 + _iota((SEL_LANES, tk), 1)) >> shift)).astype(F32))
                em = jnp.dot(selm, expand, preferred_element_type=F32)
                if nq > 1:
                    em = jnp.concatenate([em] * nq, axis=0)
                allowed = allowed & (em > 0.5)
            return _softmax_step(s, allowed, vtile, m, l, acc)

        init = (jnp.full((r, 1), NEG_INF, F32), jnp.zeros((r, 1), F32), jnp.zeros((r, LANE), F32))
        m, l, acc = lax.fori_loop(kt_lo, kt_hi, body, init)
        results.append(acc / jnp.maximum(l, 1e-30))
    _write_outputs(o_ref, jobs, results, tq)


def flash_attention(q, k, v, sel, *, jobs, out_width, tq, tk, q_pos0=0, window=0):
    nb, sq, qw = q.shape
    sk = k.shape[1]
    ops = [q, k]
    specs = [pl.BlockSpec((None, tq, qw), lambda b, i: (b, i, 0)),
             pl.BlockSpec((None, sk, k.shape[2]), lambda b, i: (b, 0, 0))]
    if v is not None:
        ops.append(v)
        specs.append(pl.BlockSpec((None, sk, v.shape[2]), lambda b, i: (b, 0, 0)))
    if sel is not None:
        ops.append(sel)
        specs.append(pl.BlockSpec((None, tq, sel.shape[2]), lambda b, i: (b, i, 0)))
    kern = functools.partial(_flash_kernel, jobs=jobs, tq=tq, tk=tk, q_pos0=q_pos0, window=window,
                             has_sel=sel is not None, has_v=v is not None)
    return pl.pallas_call(
        kern,
        grid=(nb, sq // tq),
        in_specs=specs,
        out_specs=pl.BlockSpec((None, tq, out_width), lambda b, i: (b, i, 0)),
        out_shape=jax.ShapeDtypeStruct((nb, sq, out_width), F32),
        compiler_params=_params(("parallel", "parallel")),
        name="flash_attention",
    )(*ops)


def _paged_kernel(pt_ref, *refs, jobs, npages, page_len, past_len, key_pos0, window, n_new, has_sel, has_k2,
                  has_v, k_t, k2_t, v_t):
    refs = list(refs)
    q_ref = refs.pop(0)
    sel_ref = refs.pop(0) if has_sel else None
    knew_ref = refs.pop(0)
    k2new_ref = refs.pop(0) if has_k2 else None
    vnew_ref = refs.pop(0) if has_v else knew_ref
    k_refs = [refs.pop(0) for _ in range(npages)]
    k2_refs = [refs.pop(0) for _ in range(npages)] if has_k2 else None
    v_refs = [refs.pop(0) for _ in range(npages)] if has_v else k_refs
    o_ref = refs.pop(0)
    m_scr, l_scr, acc_scr = refs
    j = pl.program_id(1)
    tq = q_ref.shape[0]
    span = npages * page_len

    @pl.when(j == 0)
    def _():
        m_scr[...] = jnp.full(m_scr.shape, NEG_INF, F32)
        l_scr[...] = jnp.zeros(l_scr.shape, F32)
        acc_scr[...] = jnp.zeros(acc_scr.shape, F32)

    cat = lambda rs, t: jnp.concatenate([r[...] for r in rs], axis=1 if t else 0)
    kcat = cat(k_refs, k_t)
    vcat = cat(v_refs, v_t) if has_v else kcat
    k2cat = _mx(cat(k2_refs, k2_t)) if has_k2 else None
    k0 = j * span
    kpos = key_pos0 + k0 + _iota((1, span), 1)
    row0 = 0
    stacked = []
    for job in jobs:
        qs, q2, slope_col = _stack_q(q_ref, job, tq)
        nq = len(job["qcols"])
        r = nq * tq
        t_rows = past_len + (_iota((r, 1), 0) & (tq - 1))
        koff, kw = job["kcol"]
        voff = job["vcol"]
        s = _dot(qs, kcat[koff:koff + kw, :]) if k_t else _dot_nt(qs, kcat[:, koff:koff + kw])
        if q2 is not None:
            s = s + (jnp.dot(q2, k2cat, preferred_element_type=F32) if k2_t
                     else lax.dot_general(q2, k2cat, _NT, preferred_element_type=F32))
        d = t_rows - kpos
        allowed = d >= 0
        if window:
            allowed = allowed & (d < window)
        if slope_col is not None:
            s = s - slope_col * d.astype(F32)
        if job["sel"] is not None and has_sel:
            soff, shift = job["sel"]
            selm = _mx(sel_ref[:, soff:soff + SEL_LANES])
            expand = _mx((_iota((SEL_LANES, span), 0) == ((k0 + _iota((SEL_LANES, span), 1)) >> shift)).astype(F32))
            em = jnp.dot(selm, expand, preferred_element_type=F32)
            if nq > 1:
                em = jnp.concatenate([em] * nq, axis=0)
            allowed = allowed & (em > 0.5)
        rows = slice(row0, row0 + r)
        vsl = vcat[voff:voff + LANE, :] if v_t else vcat[:, voff:voff + LANE]
        m, l, acc = _softmax_step(s, allowed, vsl, m_scr[rows], l_scr[rows], acc_scr[rows], v_t)
        m_scr[rows] = m
        l_scr[rows] = l
        acc_scr[rows] = acc
        stacked.append((qs, q2, slope_col, t_rows, rows, r))
        row0 += r

    @pl.when(j == pl.num_programs(1) - 1)
    def _():
        results = []
        n_rows = knew_ref.shape[0]
        cpos = _iota((1, n_rows), 1)
        for job, (qs, q2, slope_col, t_rows, rows, r) in zip(jobs, stacked):
            koff, kw = job["kcol"]
            voff = job["vcol"]
            s = _dot_nt(qs, knew_ref[:, koff:koff + kw])
            if q2 is not None:
                s = s + _dot_nt(q2, k2new_ref[...])
            d = t_rows - (past_len + cpos)
            allowed = (d >= 0) & (cpos < n_new)
            if slope_col is not None:
                s = s - slope_col * d.astype(F32)
            m, l, acc = _softmax_step(s, allowed, vnew_ref[:, voff:voff + LANE],
                                      m_scr[rows], l_scr[rows], acc_scr[rows])
            results.append(acc / jnp.maximum(l, 1e-30))
        _write_outputs(o_ref, jobs, results, tq)


def paged_attention(q, sel, k_new, k2_new, v_new, k_pool, k2_pool, v_pool, page_table, layer, *,
                    jobs, out_width, past_len, n_new, k_t, k2_t=False, v_t=False, key_pos0=0, window=0):
    nb, tq, qw = q.shape
    npg = page_table.shape[1]
    p = _largest_divisor(npg, DEC_PAGES)
    page_len = k_pool.shape[3] if k_t else k_pool.shape[2]
    if v_pool is None:
        v_t = k_t
    seq = lambda a: pl.BlockSpec((None,) + a.shape[1:], lambda b, j, pt: (b, 0, 0))
    page = lambda pool, k: pl.BlockSpec((None, None) + pool.shape[2:],
                                        lambda b, j, pt: (layer, pt[b, j * p + k], 0, 0))
    ops, specs = [q], [seq(q)]
    if sel is not None:
        ops.append(sel)
        specs.append(seq(sel))
    ops.append(k_new)
    specs.append(seq(k_new))
    if k2_new is not None:
        ops.append(k2_new)
        specs.append(seq(k2_new))
    if v_new is not None:
        ops.append(v_new)
        specs.append(seq(v_new))
    for pool in (k_pool, k2_pool, v_pool):
        if pool is not None:
            ops += [pool] * p
            specs += [page(pool, k) for k in range(p)]
    rows = sum(len(job["qcols"]) for job in jobs) * tq
    grid_spec = pltpu.PrefetchScalarGridSpec(
        num_scalar_prefetch=1,
        grid=(nb, npg // p),
        in_specs=specs,
        out_specs=pl.BlockSpec((None, tq, out_width), lambda b, j, pt: (b, 0, 0)),
        scratch_shapes=[pltpu.VMEM((rows, 1), F32), pltpu.VMEM((rows, 1), F32), pltpu.VMEM((rows, LANE), F32)],
    )
    kern = functools.partial(_paged_kernel, jobs=jobs, npages=p, page_len=page_len, past_len=past_len,
                             key_pos0=key_pos0, window=window, n_new=n_new, has_sel=sel is not None,
                             has_k2=k2_pool is not None, has_v=v_pool is not None, k_t=k_t, k2_t=k2_t, v_t=v_t)
    return pl.pallas_call(
        kern,
        grid_spec=grid_spec,
        out_shape=jax.ShapeDtypeStruct((nb, tq, out_width), F32),
        compiler_params=_params(("parallel", "arbitrary")),
        name="paged_attention",
    )(page_table, *ops)


def _merge_kernel(x_ref, sh_ref, sc_ref, gt_ref, g_ref, oc_ref, os_ref, ow_ref, gate_ref, ctx_ref, om_ref,
                  wmg_ref, wa_ref, wuv_ref, wb_ref, wc_ref, wo_ref, o_ref):
    x = x_ref[...]
    h = _modulate(x, g_ref[...], sh_ref[...], sc_ref[...])
    mg = _sigmoid(_dot(h, wmg_ref[...]))
    n_chunk = NSA_HEADS
    width = n_chunk * NSA_DH
    rowi = _iota((LANE, N_BRANCH * width), 0)
    coli = _iota((LANE, N_BRANCH * width), 1)
    expand = _mx((rowi == (coli >> _LOG2_DH)).astype(F32))
    ge = _dot_exact01(gate_ref[...], expand)
    oa = ge[:, 0:width] * oc_ref[...] + ge[:, width:2 * width] * os_ref[...] + ge[:, 2 * width:] * ow_ref[...]
    ob = _dot(ctx_ref[...], wuv_ref[...])
    d = x.shape[1]
    mixed = (mg[:, 0:d] * _dot(oa, wa_ref[...]) + mg[:, d:2 * d] * _dot(ob, wb_ref[...])
             + mg[:, 2 * d:] * _dot(om_ref[...], wc_ref[...]))
    o_ref[...] = x + gt_ref[...] * _dot(mixed, wo_ref[...])


def merge(x, mod_ops, norm_g, oc, os_, ow, gate, ctx, om, lw, tm):
    t, d = x.shape
    full = lambda a: pl.BlockSpec(a.shape, lambda i: (0,) * a.ndim)
    row = lambda a: pl.BlockSpec((tm, a.shape[1]), lambda i: (i, 0))
    (sh, sh_s), (sc, sc_s), (gt, gt_s) = mod_ops[0], mod_ops[1], mod_ops[2]
    ws = [lw["w_mg"], lw["w_br_nsa"], lw["w_uv"], lw["w_br_mla"], lw["w_br_moba"], lw["w_out"]]
    acts = [oc, os_, ow, gate, ctx, om]
    return pl.pallas_call(
        _merge_kernel,
        grid=(t // tm,),
        in_specs=[row(x), sh_s, sc_s, gt_s, full(norm_g)] + [row(a) for a in acts] + [full(w) for w in ws],
        out_specs=row(x),
        out_shape=jax.ShapeDtypeStruct((t, d), F32),
        compiler_params=_params(("parallel",)),
        name="merge",
    )(x, sh, sc, gt, norm_g, *acts, *ws)


def _ffn_kernel(x_ref, sh_ref, sc_ref, gt_ref, g_ref, wi_ref, wo_ref, fn_ref, o_ref, *, final):
    x = x_ref[...]
    h = _modulate(x, g_ref[...], sh_ref[...], sc_ref[...])
    au = _dot(h, wi_ref[...])
    hid = au.shape[1] // 2
    y = x + gt_ref[...] * _dot(_silu(au[:, :hid]) * au[:, hid:], wo_ref[...])
    o_ref[...] = _rms(y, fn_ref[...]) if final else y


def ffn(x, mod_ops, norm_g, lw, final_norm, final, tm):
    t, d = x.shape
    full = lambda a: pl.BlockSpec(a.shape, lambda i: (0,) * a.ndim, pipeline_mode=pl.Buffered(1))
    row = pl.BlockSpec((tm, d), lambda i: (i, 0))
    (sh, sh_s), (sc, sc_s), (gt, gt_s) = mod_ops[3], mod_ops[4], mod_ops[5]
    return pl.pallas_call(
        functools.partial(_ffn_kernel, final=final),
        grid=(t // tm,),
        in_specs=[row, sh_s, sc_s, gt_s, full(norm_g), full(lw["ffn_in"]), full(lw["ffn_out"]), full(final_norm)],
        out_specs=row,
        out_shape=jax.ShapeDtypeStruct((t, d), F32),
        compiler_params=_params(("parallel",)),
        name="ffn",
    )(x, sh, sc, gt, norm_g, lw["ffn_in"], lw["ffn_out"], final_norm)


def _prep_layer(l, w_in, mla_q_norm, mla_w_uq, mla_kv_norm, mla_w_uk, mla_w_uv, nsa_cmp_w1, nsa_cmp_pe,
                nsa_cmp_w2, w_br_nsa, w_br_mla, w_br_moba, w_out, ffn_w_in, ffn_w_out):
    d = w_in.shape[1]
    g, hpg, dh = NSA_GROUPS, NSA_HPG, NSA_DH
    cuts = np.cumsum(IN_WIDTHS)[:-1].tolist()
    nq, nkv, ng, mql, mkvl, mkpe, mqkv, mg = jnp.split(w_in[l], cuts, axis=1)
    nq_p = nq.reshape(d, g, hpg, dh).transpose(0, 2, 1, 3).reshape(d, g * hpg * dh)
    ng_p = ng.reshape(d, g, hpg, N_BRANCH).transpose(0, 3, 2, 1).reshape(d, N_BRANCH * g * hpg)
    ng_p = jnp.pad(ng_p, ((0, 0), (0, LANE - ng_p.shape[1])))
    half = MLA_ROPE // 2
    swap = lambda a: jnp.concatenate([a[..., half:], a[..., :half]], axis=-1)
    rep = LANE // MLA_ROPE
    f1 = jnp.tile(mkpe, (1, rep))
    f2 = jnp.tile(swap(mkpe), (1, rep))
    w_a = jnp.concatenate([nq_p, nkv, ng_p, mql, mkvl, f1, f2, mqkv], axis=1)
    uq = mla_w_uq[l]
    uq_nope = uq[:, :, :MLA_NOPE].reshape(MLA_Q_LORA, MLA_HEADS * MLA_NOPE)
    uq_pe = uq[:, :, MLA_NOPE:]
    w_uq = jnp.concatenate([uq_nope, uq_pe.reshape(MLA_Q_LORA, -1), swap(uq_pe).reshape(MLA_Q_LORA, -1)], axis=1)
    uk = mla_w_uk[l]
    w_uk = block_diag(*[uk[:, hd, :].T for hd in range(MLA_HEADS)])
    uv = mla_w_uv[l]
    w_uv = block_diag(*[uv[:, hd, :] for hd in range(MLA_HEADS)])
    w1 = nsa_cmp_w1[l]
    hid = w1.shape[-1]
    wc = jnp.zeros((NSA_CMP_STRIDE, 2, g, dh, NSA_CMP_R, 2, g, hid), F32)
    for kv in range(2):
        for gi in range(g):
            wc = wc.at[:, kv, gi, :, :, kv, gi, :].set(w1[kv].transpose(1, 2, 0, 3))
    w_chunks = wc.reshape(NSA_CMP_STRIDE, 2 * g * dh, NSA_CMP_R * 2 * g * hid)
    pe = nsa_cmp_pe[l].reshape(2, NSA_CMP_R, NSA_CMP_STRIDE, dh)
    pe_h = jnp.einsum('krcd,krcdh->kh', pe, w1, precision=lax.Precision.HIGHEST)
    pe_const = jnp.concatenate([pe_h[kv] for kv in range(2) for _ in range(g)])[None, :]
    w2 = nsa_cmp_w2[l]
    w2_bd = block_diag(*[w2[kv] for kv in range(2) for _ in range(g)])
    br_nsa = w_br_nsa[l].reshape(g, hpg, dh, d).transpose(1, 0, 2, 3).reshape(g * hpg * dh, d)
    return dict(
        w_a=_mx(w_a), qn=mla_q_norm[l][None, :], w_uq=_mx(w_uq), w_uk=_mx(w_uk), kvn=mla_kv_norm[l][None, :],
        w_chunks=_mx(w_chunks), pe_const=pe_const, w2_bd=_mx(w2_bd), w_mg=_mx(mg), w_br_nsa=_mx(br_nsa),
        w_uv=_mx(w_uv), w_br_mla=_mx(w_br_mla[l]), w_br_moba=_mx(w_br_moba[l]), w_out=_mx(w_out[l]),
        ffn_in=_mx(ffn_w_in[l]), ffn_out=_mx(ffn_w_out[l]))


def _rope_tables(pos):
    half = MLA_ROPE // 2
    inv = ROPE_THETA ** (-jnp.arange(half, dtype=F32) / half)
    ang = pos.astype(F32)[:, None] * inv
    cos, sin = jnp.cos(ang), jnp.sin(ang)
    rep = LANE // MLA_ROPE
    return (jnp.tile(jnp.concatenate([cos, cos], axis=1), (1, rep)),
            jnp.tile(jnp.concatenate([-sin, sin], axis=1), (1, rep)))


def _pad_rows(a, rows):
    return jnp.pad(a, ((0, 0), (0, rows - a.shape[1]), (0, 0)))


def kernel(x_prompt, x_sample, c_prompt, c_sample, cache_nsa_cmp_kv, cache_nsa_slc_k, cache_nsa_slc_v, state_nsa_win_kv, cache_mla_latent, cache_mla_kpe, cache_moba_k, cache_moba_v, page_table, ada_w, ada_b, norm_mix, norm_ffn, w_in, nsa_cmp_w1, nsa_cmp_pe, nsa_cmp_w2, mla_q_norm, mla_w_uq, mla_kv_norm, mla_w_uk, mla_w_uv, w_br_nsa, w_br_mla, w_br_moba, w_out, ffn_w_in, ffn_w_out, final_norm):
    depth = w_in.shape[0]
    nb, seq, d = x_prompt.shape
    db, ds, _ = x_sample.shape
    n_pool = cache_nsa_slc_k.shape[1]
    npg = page_table.shape[1]
    past_len = npg * PAGE_SIZE
    win_len = state_nsa_win_kv.shape[2]
    tp, ts = nb * seq, db * ds
    tm = TOKEN_TILE
    tm_s = min(TOKEN_TILE, ts)
    assert seq % MOBA_BLOCK == 0 and seq % ATT_TK == 0 and seq % tm == 0 and ts % tm_s == 0 and tm_s % SUBLANE == 0
    assert past_len % MOBA_BLOCK == 0 and ds <= DEC_Q and win_len == NSA_WINDOW <= past_len
    assert past_len // NSA_SEL_BLOCK <= SEL_LANES and seq // NSA_SEL_BLOCK <= SEL_LANES
    nsa_sl, moba_sl = alibi_slopes()
    nsa_jobs, moba_jobs = _nsa_jobs(nsa_sl), _moba_jobs(moba_sl)
    win_jobs = [dict(j, sel=None, vcol=LANE) for j in nsa_jobs]
    mla_jobs = [dict(qcols=[(hd * 256, 256) for hd in range(MLA_HEADS)], qmask=None, q2cols=None, kcol=(0, 256),
                     vcol=0, slopes=None, sel=None, outs=[(hd * LANE, 0, LANE) for hd in range(MLA_HEADS)])]
    mla_dec_jobs = [dict(mla_jobs[0], qcols=[(hd * 256, LANE) for hd in range(MLA_HEADS)], kcol=(0, LANE),
                         q2cols=[(hd * 256 + LANE + hd * MLA_ROPE, MLA_ROPE) for hd in range(MLA_HEADS)])]

    n_c = nb + db
    c_all = jnp.pad(jnp.concatenate([c_prompt, c_sample], axis=0), ((0, -n_c % SUBLANE), (0, 0)))
    mod_all = adaln_all(c_all, ada_w, ada_b)

    def mod_ops_prompt(l):
        arr = mod_all[l, :nb].reshape(nb * N_MOD, 1, d)
        tiles = seq // tm
        return [(arr, pl.BlockSpec((None, 1, d), lambda i, k=k: ((i // tiles) * N_MOD + k, 0, 0)))
                for k in range(N_MOD)]

    def mod_ops_sample(l):
        arr = jnp.repeat(mod_all[l, nb:n_c].reshape(db, N_MOD, d), ds, axis=0).transpose(1, 0, 2)
        return [(arr, pl.BlockSpec((None, tm_s, d), lambda i, k=k: (k, i, 0))) for k in range(N_MOD)]

    c4p, s4p = _rope_tables(jnp.arange(seq))
    c4s, s4s = _rope_tables(jnp.tile(past_len + jnp.arange(ds), db))
    tiles_p = seq // tm
    tbl_spec_p = pl.BlockSpec((tm, LANE), lambda i: (i % tiles_p, 0))
    tbl_spec_s = pl.BlockSpec((tm_s, LANE), lambda i: (i, 0))

    def pages_t(a):
        nd = a.ndim
        return jnp.transpose(a, (0, 1) + tuple(range(3, nd)) + (2,)).reshape(a.shape[0], a.shape[1], -1, a.shape[2])

    cmp_pool = pages_t(cache_nsa_cmp_kv)
    slc_k_pool, slc_v_pool = pages_t(cache_nsa_slc_k), pages_t(cache_nsa_slc_v)
    moba_k_pool, moba_v_pool = pages_t(cache_moba_k), pages_t(cache_moba_v)
    kpe_pool = pages_t(cache_mla_kpe)
    win_pool = pages_t(state_nsa_win_kv)
    ident_win = jnp.arange(db, dtype=jnp.int32).reshape(db, 1)
    ident_pt = jnp.arange(nb * (seq // PAGE_SIZE), dtype=jnp.int32).reshape(nb, seq // PAGE_SIZE)
    cpp = PAGE_SIZE // NSA_CMP_STRIDE

    xp = x_prompt.reshape(tp, d)
    xs = x_sample.reshape(ts, d)
    st_p, st_s = [], []
    for l in range(depth):
        lw = _prep_layer(l, w_in, mla_q_norm, mla_w_uq, mla_kv_norm, mla_w_uk, mla_w_uv, nsa_cmp_w1, nsa_cmp_pe,
                         nsa_cmp_w2, w_br_nsa, w_br_mla, w_br_moba, w_out, ffn_w_in, ffn_w_out)
        g_mix, g_ffn = norm_mix[l][None, :], norm_ffn[l][None, :]
        last = l == depth - 1
        fin = final_norm[None, :]

        mods = mod_ops_prompt(l)
        (q, cmp, sk, sv, win, gate, qmla, latkpe, mq, mk, mv) = inproj(xp, mods, g_mix, lw, c4p, s4p, tbl_spec_p, tm)
        b3 = lambda a: a.reshape(nb, seq, a.shape[-1])
        a_chunks = cmp_chunk_proj(cmp.reshape(1, tp // PAGE_SIZE, PAGE_SIZE, cmp.shape[-1]), 0, lw["w_chunks"])
        kcvc = cmp_assemble(a_chunks.reshape(tp // PAGE_SIZE, cpp, -1), ident_pt, lw["pe_const"], lw["w2_bd"])
        oc, sel = nsa_cmp_select(b3(q), kcvc, tq=ATT_TQ, t0=0, nc=seq // NSA_CMP_STRIDE - 1,
                                 ns=seq // NSA_SEL_BLOCK, own_in_range=True, slopes=nsa_sl)
        os_ = flash_attention(b3(q), b3(sk), b3(sv), sel, jobs=nsa_jobs, out_width=512, tq=ATT_TQ, tk=ATT_TK)
        ow = flash_attention(b3(q), b3(win), None, None, jobs=win_jobs, out_width=512, tq=ATT_TQ, tk=ATT_TK,
                             window=NSA_WINDOW)
        ctx = flash_attention(b3(qmla), b3(latkpe), None, None, jobs=mla_jobs, out_width=512, tq=ATT_TQ, tk=ATT_TK)
        msel = moba_select(b3(mq), moba_block_mean(b3(mk)), tq=ATT_TQ, t0=0, n_full=seq // MOBA_BLOCK,
                           own_in_range=True)
        om = flash_attention(b3(mq), b3(mk), b3(mv), msel, jobs=moba_jobs, out_width=256, tq=ATT_TQ, tk=ATT_TK)
        f2 = lambda a: a.reshape(tp, a.shape[-1])
        x1 = merge(xp, mods, g_mix, f2(oc), f2(os_), f2(ow), gate, f2(ctx), f2(om), lw, tm)
        xp = ffn(x1, mods, g_ffn, lw, fin, last, tm)
        win_keep = min(NSA_WINDOW, seq)
        st_p.append((cmp.reshape(nb, seq, 2, NSA_GROUPS, NSA_DH), sk.reshape(nb, seq, NSA_GROUPS, NSA_DH),
                     sv.reshape(nb, seq, NSA_GROUPS, NSA_DH),
                     b3(win)[:, seq - win_keep:].reshape(nb, win_keep, 2, NSA_GROUPS, NSA_DH),
                     b3(latkpe)[:, :, :MLA_KV_LORA], b3(latkpe)[:, :, LANE:LANE + MLA_ROPE],
                     mk.reshape(nb, seq, MOBA_HEADS, MOBA_DH), mv.reshape(nb, seq, MOBA_HEADS, MOBA_DH)))

        mods = mod_ops_sample(l)
        (q, cmp, sk, sv, win, gate, qmla, latkpe, mq, mk, mv) = inproj(xs, mods, g_mix, lw, c4s, s4s, tbl_spec_s, tm_s)
        s3 = lambda a: a.reshape(db, ds, a.shape[-1])
        q8, qmla8, mq8 = (_pad_rows(s3(a), DEC_Q) for a in (q, qmla, mq))
        new16 = lambda a: _pad_rows(s3(a), DEC_NEW)
        a_pool = cmp_chunk_proj_t(cmp_pool, l, lw["w_chunks"])
        kcvc = cmp_assemble(a_pool.reshape(n_pool, cpp, -1), page_table, lw["pe_const"], lw["w2_bd"])
        oc8, sel8 = nsa_cmp_select(q8, kcvc, tq=DEC_Q, t0=past_len, nc=past_len // NSA_CMP_STRIDE - 1,
                                   ns=past_len // NSA_SEL_BLOCK, own_in_range=False, slopes=nsa_sl)
        os8 = paged_attention(q8, sel8, new16(sk), None, new16(sv), slc_k_pool, None, slc_v_pool, page_table, l,
                              jobs=nsa_jobs, out_width=512, past_len=past_len, n_new=ds, k_t=True, v_t=True)
        ow8 = paged_attention(q8, None, new16(win), None, None, win_pool, None, None, ident_win, l,
                              jobs=win_jobs, out_width=512, past_len=past_len, n_new=ds, k_t=True,
                              key_pos0=past_len - win_len, window=NSA_WINDOW)
        lat_new = new16(latkpe)
        ctx8 = paged_attention(qmla8, None, lat_new[:, :, :LANE], lat_new[:, :, LANE:LANE + MLA_ROPE], None,
                               cache_mla_latent, kpe_pool, None, page_table, l,
                               jobs=mla_dec_jobs, out_width=512, past_len=past_len, n_new=ds, k_t=False, k2_t=True)
        kmean = moba_kmean_paged(page_sum(moba_k_pool, l), page_table)
        msel8 = moba_select(mq8, kmean, tq=DEC_Q, t0=past_len, n_full=past_len // MOBA_BLOCK, own_in_range=False)
        om8 = paged_attention(mq8, msel8, new16(mk), None, new16(mv), moba_k_pool, None, moba_v_pool, page_table, l,
                              jobs=moba_jobs, out_width=256, past_len=past_len, n_new=ds, k_t=True, v_t=True)
        win_buf = state_nsa_win_kv[l].reshape(db, win_len, 2 * NSA_GROUPS * NSA_DH)
        win_cat = jnp.concatenate([win_buf, s3(win)], axis=1)
        d2 = lambda a: a[:, :ds].reshape(ts, a.shape[-1])
        x1 = merge(xs, mods, g_mix, d2(oc8), d2(os8), d2(ow8), gate, d2(ctx8), d2(om8), lw, tm_s)
        xs = ffn(x1, mods, g_ffn, lw, fin, last, tm_s)
        keep = min(NSA_WINDOW, past_len + ds)
        new_win = win_cat[:, -keep:].reshape(db, keep, 2, NSA_GROUPS, NSA_DH)
        st_s.append((cmp.reshape(db, ds, 2, NSA_GROUPS, NSA_DH), sk.reshape(db, ds, NSA_GROUPS, NSA_DH),
                     sv.reshape(db, ds, NSA_GROUPS, NSA_DH), new_win,
                     s3(latkpe)[:, :, :MLA_KV_LORA], s3(latkpe)[:, :, LANE:LANE + MLA_ROPE],
                     mk.reshape(db, ds, MOBA_HEADS, MOBA_DH), mv.reshape(db, ds, MOBA_HEADS, MOBA_DH)))

    outs_p = [jnp.stack(a, axis=0) for a in zip(*st_p)]
    outs_s = [jnp.stack(a, axis=0) for a in zip(*st_s)]
    return (xp.reshape(nb, seq, d), xs.reshape(db, ds, d), *outs_p, *outs_s)
```

```python
import functools

import jax
import jax.numpy as jnp
from jax import lax
import numpy as np
from jax.experimental import pallas as pl
from jax.experimental.pallas import tpu as pltpu
from jax.scipy.linalg import block_diag

D_MODEL = 1024
PAGE_SIZE = 128
NSA_HEADS = 8
NSA_GROUPS = 2
NSA_HPG = NSA_HEADS // NSA_GROUPS
NSA_DH = 64
NSA_CMP_LEN = 32
NSA_CMP_STRIDE = 16
NSA_CMP_R = NSA_CMP_LEN // NSA_CMP_STRIDE
NSA_CMP_HID = 2 * NSA_DH
NSA_SEL_BLOCK = 64
NSA_TOPN = 16
NSA_WINDOW = 512
MLA_HEADS = 4
MLA_Q_LORA = 256
MLA_KV_LORA = 128
MLA_NOPE = 64
MLA_ROPE = 32
MLA_V = 64
MLA_SCALE = (MLA_NOPE + MLA_ROPE) ** -0.5
ROPE_THETA = 10000.0
MOBA_HEADS = 4
MOBA_DH = 64
MOBA_BLOCK = 256
MOBA_TOPK = 3
FFN_HIDDEN = ((8 * D_MODEL + 3 * 256 - 1) // (3 * 256)) * 256
N_BRANCH = 3
N_MOD = 6
N_ALIBI = NSA_HEADS + MOBA_HEADS
RMS_EPS = 1e-6
IN_WIDTHS = (NSA_HEADS * NSA_DH, N_BRANCH * 2 * NSA_GROUPS * NSA_DH, N_BRANCH * NSA_HEADS,
             MLA_Q_LORA, MLA_KV_LORA, MLA_ROPE, 3 * MOBA_HEADS * MOBA_DH, N_BRANCH * D_MODEL)
NSA_SCALE = NSA_DH ** -0.5
MOBA_SCALE = MOBA_DH ** -0.5
_LOG2_DH = 6
_LOG2_SEL = 6
_LOG2_MOBA = 8
_LOG2_ROPE = 5
assert (1 << _LOG2_DH == NSA_DH == MOBA_DH and 1 << _LOG2_SEL == NSA_SEL_BLOCK
        and 1 << _LOG2_MOBA == MOBA_BLOCK and 1 << _LOG2_ROPE == MLA_ROPE)

F32 = jnp.float32
MXU_DTYPE = jnp.bfloat16
LANE = 128
SUBLANE = 8
VMEM_LIMIT = 56 * 1024 * 1024
TOKEN_TILE = 256
ATT_TQ = 128
ATT_TK = 256
ATT_TK_LONG = 512
DEC_Q = 8
DEC_NEW = 16
DEC_PAGES = 32
DEC_SEQS = 8
ASM_PAGES = 32
CMP_PAGES = 32
SEL_LANES = LANE
NEG_INF = float("-inf")


def _params(sem):
    return pltpu.CompilerParams(dimension_semantics=sem, vmem_limit_bytes=VMEM_LIMIT)


def _mx(x):
    return x.astype(MXU_DTYPE)


def _dot(a, b):
    return jnp.dot(_mx(a), _mx(b), preferred_element_type=F32)


_NT = (((1,), (1,)), ((), ()))


def _dot_nt(a, b):
    return lax.dot_general(_mx(a), _mx(b), _NT, preferred_element_type=F32)


def _split2(x):
    hi = _mx(x)
    lo = _mx(x - hi.astype(F32))
    return hi, lo


def _dot3_nt(a, b):
    ah, al = _split2(a)
    bh, bl = _split2(b)
    f = lambda x, y: lax.dot_general(x, y, _NT, preferred_element_type=F32)
    return f(ah, bh) + (f(ah, bl) + f(al, bh))


def _dot_exact01(x, e):
    h1 = _mx(x)
    r1 = x - h1.astype(F32)
    h2 = _mx(r1)
    h3 = _mx(r1 - h2.astype(F32))
    f = lambda a: jnp.dot(a, e, preferred_element_type=F32)
    return f(h1) + (f(h2) + f(h3))


def _sigmoid(x):
    return 1.0 / (1.0 + jnp.exp(-x))


def _silu(x):
    return x * _sigmoid(x)


def _rms(x, g):
    return x * lax.rsqrt(jnp.mean(x * x, axis=-1, keepdims=True) + RMS_EPS) * g


def _modulate(x, g, shift, scale):
    return _rms(x, g) * (1.0 + scale) + shift


def _iota(shape, dim):
    return lax.broadcasted_iota(jnp.int32, shape, dim)


def _topk_mask(vals, cand, k):
    lane = _iota(vals.shape, 1).astype(F32)
    cur = jnp.where(cand, vals, NEG_INF)
    sel = jnp.zeros(vals.shape, F32)
    for _ in range(k):
        mx = jnp.max(cur, axis=-1, keepdims=True)
        hit = (cur == mx) & (cur > NEG_INF)
        idx = jnp.min(jnp.where(hit, lane, 1e9), axis=-1, keepdims=True)
        one = lane == idx
        sel = jnp.where(one, 1.0, sel)
        cur = jnp.where(one, NEG_INF, cur)
    return sel


def alibi_slopes():
    s = 2.0 ** (-8.0 * np.arange(1, N_ALIBI + 1) / N_ALIBI)
    step = N_ALIBI // MOBA_HEADS
    moba_idx = np.arange(MOBA_HEADS) * step + step - 1
    nsa_idx = np.setdiff1d(np.arange(N_ALIBI), moba_idx)
    nsa = np.asarray(s[nsa_idx], np.float32).reshape(NSA_GROUPS, NSA_HPG)
    moba = np.asarray(s[moba_idx], np.float32)
    return [[float(v) for v in row] for row in nsa], [float(v) for v in moba]


def _adaln_kernel(c_ref, w_ref, b_ref, o_ref):
    o_ref[...] = _dot(_silu(c_ref[...]), w_ref[...]) + b_ref[...]


def adaln_all(c_all, ada_w, ada_b):
    depth, d, n = ada_w.shape
    r = c_all.shape[0]
    tn = 1024
    return pl.pallas_call(
        _adaln_kernel,
        grid=(depth, n // tn),
        in_specs=[pl.BlockSpec((r, d), lambda l, j: (0, 0)),
                  pl.BlockSpec((None, d, tn), lambda l, j: (l, 0, j)),
                  pl.BlockSpec((None, 1, tn), lambda l, j: (l, 0, j))],
        out_specs=pl.BlockSpec((None, r, tn), lambda l, j: (l, 0, j)),
        out_shape=jax.ShapeDtypeStruct((depth, r, n), F32),
        compiler_params=_params(("parallel", "parallel")),
        name="adaln",
    )(c_all, _mx(ada_w), ada_b.reshape(depth, 1, n))


_C_Q, _C_CMP, _C_SK, _C_SV, _C_WIN, _C_GATE, _C_MQL, _C_MKV, _C_F1, _C_F2, _C_MOBA, _C_END = (
    0, 512, 768, 896, 1024, 1280, 1408, 1664, 1792, 1920, 2048, 2816)


def _inproj_kernel(x_ref, sh_ref, sc_ref, g_ref, wa_ref, qn_ref, wuq_ref, wuk_ref, kvn_ref, c4_ref, s4_ref,
                   oq, ocmp, osk, osv, owin, ogate, oqmla, olatkpe, omq, omk, omv):
    h = _modulate(x_ref[...], g_ref[...], sh_ref[...], sc_ref[...])
    y = _dot(h, wa_ref[...])
    oq[...] = y[:, _C_Q:_C_CMP] * NSA_SCALE
    ocmp[...] = y[:, _C_CMP:_C_SK]
    osk[...] = y[:, _C_SK:_C_SV]
    osv[...] = y[:, _C_SV:_C_WIN]
    owin[...] = y[:, _C_WIN:_C_GATE]
    ogate[...] = _sigmoid(y[:, _C_GATE:_C_MQL])
    c4 = c4_ref[...]
    s4 = s4_ref[...]
    olatkpe[:, 0:LANE] = _rms(y[:, _C_MKV:_C_F1], kvn_ref[...])
    olatkpe[:, LANE:2 * LANE] = y[:, _C_F1:_C_F2] * c4 + y[:, _C_F2:_C_MOBA] * s4
    mq = _dot(_rms(y[:, _C_MQL:_C_MKV], qn_ref[...]), wuq_ref[...])
    qpe = (mq[:, 256:384] * c4 + mq[:, 384:512] * s4) * MLA_SCALE
    qlat = _dot(mq[:, 0:256], wuk_ref[...]) * MLA_SCALE
    head_of_lane = _iota(qpe.shape, 1) >> _LOG2_ROPE
    for hd in range(MLA_HEADS):
        oqmla[:, hd * 256:hd * 256 + LANE] = qlat[:, hd * LANE:(hd + 1) * LANE]
        oqmla[:, hd * 256 + LANE:(hd + 1) * 256] = jnp.where(head_of_lane == hd, qpe, 0.0)
    omq[...] = y[:, _C_MOBA:_C_MOBA + 256] * MOBA_SCALE
    omk[...] = y[:, _C_MOBA + 256:_C_MOBA + 512]
    omv[...] = y[:, _C_MOBA + 512:_C_END]


_INPROJ_WIDTHS = (512, 256, 128, 128, 256, 128, 1024, 256, 256, 256, 256)


def inproj(x, mod_ops, norm_g, lw, c4, s4, table_spec, tm):
    t, d = x.shape
    (sh, sh_spec), (sc, sc_spec) = mod_ops[0], mod_ops[1]
    full = lambda a: pl.BlockSpec(a.shape, lambda i: (0,) * a.ndim)
    row = lambda w: pl.BlockSpec((tm, w), lambda i: (i, 0))
    return pl.pallas_call(
        _inproj_kernel,
        grid=(t // tm,),
        in_specs=[row(d), sh_spec, sc_spec, full(norm_g), full(lw["w_a"]), full(lw["qn"]), full(lw["w_uq"]),
                  full(lw["w_uk"]), full(lw["kvn"]), table_spec, table_spec],
        out_specs=[row(w) for w in _INPROJ_WIDTHS],
        out_shape=[jax.ShapeDtypeStruct((t, w), F32) for w in _INPROJ_WIDTHS],
        compiler_params=_params(("parallel",)),
        name="inproj",
    )(x, sh, sc, norm_g, lw["w_a"], lw["qn"], lw["w_uq"], lw["w_uk"], lw["kvn"], c4, s4)


def _cmp_a_kernel(xa_ref, xb_ref, w_ref, o_ref, *, pp):
    cpp = PAGE_SIZE // NSA_CMP_STRIDE
    acc = None
    for c in range(NSA_CMP_STRIDE):
        rows = pl.ds(c, cpp, stride=NSA_CMP_STRIDE)
        xc = jnp.concatenate([xa_ref[:, rows, :].reshape(pp * cpp, LANE),
                              xb_ref[:, rows, :].reshape(pp * cpp, LANE)], axis=1)
        part = _dot(xc, w_ref[c])
        acc = part if acc is None else acc + part
    o_ref[...] = acc


def _cmp_a_t_kernel(xt_ref, w_ref, o_ref, xa_scr, xb_scr, *, pp):
    for p in range(pp):
        x = xt_ref[p].T
        xa_scr[p] = x[:, 0:LANE]
        xb_scr[p] = x[:, LANE:2 * LANE]
    _cmp_a_kernel(xa_scr, xb_scr, w_ref, o_ref, pp=pp)


def cmp_chunk_proj_t(pages_t, layer, w_chunks):
    _, npg, width, ps = pages_t.shape
    cpp = ps // NSA_CMP_STRIDE
    pp = _largest_divisor(npg, CMP_PAGES)
    nout = w_chunks.shape[-1]
    return pl.pallas_call(
        functools.partial(_cmp_a_t_kernel, pp=pp),
        grid=(npg // pp,),
        in_specs=[pl.BlockSpec((None, pp, width, ps), lambda i: (layer, i, 0, 0)),
                  pl.BlockSpec(w_chunks.shape, lambda i: (0, 0, 0))],
        out_specs=pl.BlockSpec((pp * cpp, nout), lambda i: (i, 0)),
        out_shape=jax.ShapeDtypeStruct((npg * cpp, nout), F32),
        scratch_shapes=[pltpu.VMEM((pp, ps, LANE), F32), pltpu.VMEM((pp, ps, LANE), F32)],
        compiler_params=_params(("parallel",)),
        name="cmp_chunk_proj_t",
    )(pages_t, w_chunks)


def _largest_divisor(n, cap):
    for p in range(min(cap, n), 0, -1):
        if n % p == 0:
            return p
    return 1


def cmp_chunk_proj(pages, layer, w_chunks):
    _, npg, ps, width = pages.shape
    cpp = ps // NSA_CMP_STRIDE
    pp = _largest_divisor(npg, CMP_PAGES)
    nout = w_chunks.shape[-1]
    return pl.pallas_call(
        functools.partial(_cmp_a_kernel, pp=pp),
        grid=(npg // pp,),
        in_specs=[pl.BlockSpec((None, pp, ps, LANE), lambda i: (layer, i, 0, 0)),
                  pl.BlockSpec((None, pp, ps, LANE), lambda i: (layer, i, 0, 1)),
                  pl.BlockSpec(w_chunks.shape, lambda i: (0, 0, 0))],
        out_specs=pl.BlockSpec((pp * cpp, nout), lambda i: (i, 0)),
        out_shape=jax.ShapeDtypeStruct((npg * cpp, nout), F32),
        compiler_params=_params(("parallel",)),
        name="cmp_chunk_proj",
    )(pages, pages, w_chunks)


def _cmp_asm_kernel(pt_ref, *refs, npages):
    a_refs = refs[:npages + 1]
    pe_ref, w2_ref, o_ref = refs[npages + 1:]
    rows = jnp.concatenate([r[...] for r in a_refs], axis=0)
    n = npages * SUBLANE
    half = rows.shape[1] // 2
    nxt = pltpu.roll(rows[:, half:], rows.shape[0] - 1, 0)
    hid = rows[0:n, 0:half] + nxt[0:n, :] + pe_ref[...]
    o_ref[...] = _dot(_silu(hid), w2_ref[...])


def cmp_assemble(a_pages, page_table, pe_const, w2_bd):
    nb, npg = page_table.shape
    p = _largest_divisor(npg, ASM_PAGES)
    cpp, wa = a_pages.shape[1:]
    a_spec = lambda k: pl.BlockSpec(
        (None, cpp, wa), lambda b, j, pt: (pt[b, jnp.minimum(j * p + k, npg - 1)], 0, 0))
    grid_spec = pltpu.PrefetchScalarGridSpec(
        num_scalar_prefetch=1,
        grid=(nb, npg // p),
        in_specs=[a_spec(k) for k in range(p + 1)]
        + [pl.BlockSpec(pe_const.shape, lambda b, j, pt: (0, 0)),
           pl.BlockSpec(w2_bd.shape, lambda b, j, pt: (0, 0))],
        out_specs=pl.BlockSpec((None, p * cpp, w2_bd.shape[1]), lambda b, j, pt: (b, j, 0)),
    )
    return pl.pallas_call(
        functools.partial(_cmp_asm_kernel, npages=p),
        grid_spec=grid_spec,
        out_shape=jax.ShapeDtypeStruct((nb, npg * cpp, w2_bd.shape[1]), F32),
        compiler_params=_params(("parallel", "parallel")),
        name="cmp_assemble",
    )(page_table, *([a_pages] * (p + 1)), pe_const, w2_bd)


def _nsa_cmp_kernel(q_ref, kcvc_ref, oc_ref, sel_ref, *, bb, tq, t0, nc, ns, n_top, own_in_range, slopes):
    tbase = t0 + pl.program_id(1) * tq
    ncp = kcvc_ref.shape[1]
    r = NSA_GROUPS * NSA_HPG * tq
    lane = _iota((tq, LANE), 1)
    t_rows = tbase + (_iota((r, 1), 0) & (tq - 1))
    cidx = _iota((1, ncp), 1)
    d_c = t_rows - (cidx * NSA_CMP_STRIDE + (NSA_CMP_LEN - 1))
    valid = (d_c >= 0) & (cidx < nc)
    d_cf = d_c.astype(F32)
    ci = _iota((ncp, SEL_LANES), 0) * NSA_CMP_STRIDE
    sj = _iota((ncp, SEL_LANES), 1) * NSA_SEL_BLOCK
    cover = ((ci < sj + NSA_SEL_BLOCK) & (ci + NSA_CMP_LEN > sj)
             & (_iota((ncp, SEL_LANES), 0) < nc) & (_iota((ncp, SEL_LANES), 1) < ns))
    cover = _mx(cover.astype(F32))
    halves = [(lane >= g * NSA_DH) & (lane < (g + 1) * NSA_DH) for g in range(NSA_GROUPS)]
    slope_col = jnp.concatenate([jnp.full((tq, 1), s, F32) for g in range(NSA_GROUPS) for s in slopes[g]], axis=0)
    imps = []
    for b in range(bb):
        kc = kcvc_ref[b, :, 0:LANE]
        vc = kcvc_ref[b, :, LANE:2 * LANE]
        rows = jnp.concatenate([jnp.where(halves[g], q_ref[b, :, c * LANE:(c + 1) * LANE], 0.0)
                                for g in range(NSA_GROUPS) for c in range(NSA_HPG)], axis=0)
        s = _dot3_nt(rows, kc) - slope_col * d_cf
        s = jnp.where(valid, s, NEG_INF)
        m = jnp.max(s, axis=-1, keepdims=True)
        m = jnp.where(m > NEG_INF, m, 0.0)
        e = jnp.where(valid, jnp.exp(s - m), 0.0)
        p = e / jnp.maximum(jnp.sum(e, axis=-1, keepdims=True), 1e-30)
        o = _dot(p, vc)
        for c in range(NSA_HPG):
            oc_ref[b, :, c * LANE:(c + 1) * LANE] = jnp.where(
                lane < NSA_DH, o[c * tq:(c + 1) * tq], o[(NSA_HPG + c) * tq:(NSA_HPG + c + 1) * tq])
        for g in range(NSA_GROUPS):
            psum = p[g * NSA_HPG * tq:(g * NSA_HPG + 1) * tq]
            for c in range(1, NSA_HPG):
                psum = psum + p[(g * NSA_HPG + c) * tq:(g * NSA_HPG + c + 1) * tq]
            ph, plo = _split2(psum)
            imps.append(jnp.dot(ph, cover, preferred_element_type=F32)
                        + jnp.dot(plo, cover, preferred_element_type=F32))
    imp = jnp.concatenate(imps, axis=0)
    n_sets = bb * NSA_GROUPS
    lane_all = _iota((n_sets * tq, SEL_LANES), 1)
    blk_q = (tbase + (_iota((n_sets * tq, 1), 0) & (tq - 1))) >> _LOG2_SEL
    sel = _topk_mask(imp, (lane_all < blk_q) & (lane_all < ns), n_top)
    if own_in_range:
        sel = jnp.where(lane_all == blk_q, 1.0, sel)
    for b in range(bb):
        for g in range(NSA_GROUPS):
            k = b * NSA_GROUPS + g
            sel_ref[b, :, g * SEL_LANES:(g + 1) * SEL_LANES] = sel[k * tq:(k + 1) * tq]


def nsa_cmp_select(q, kcvc, *, bb, tq, t0, nc, ns, own_in_range, slopes):
    nb, sq, qw = q.shape
    ncp = kcvc.shape[1]
    kern = functools.partial(_nsa_cmp_kernel, bb=bb, tq=tq, t0=t0, nc=nc, ns=ns, n_top=min(NSA_TOPN, ns),
                             own_in_range=own_in_range, slopes=slopes)
    return pl.pallas_call(
        kern,
        grid=(nb // bb, sq // tq),
        in_specs=[pl.BlockSpec((bb, tq, qw), lambda b, i: (b, i, 0)),
                  pl.BlockSpec((bb, ncp, kcvc.shape[2]), lambda b, i: (b, 0, 0))],
        out_specs=[pl.BlockSpec((bb, tq, qw), lambda b, i: (b, i, 0)),
                   pl.BlockSpec((bb, tq, NSA_GROUPS * SEL_LANES), lambda b, i: (b, i, 0))],
        out_shape=[jax.ShapeDtypeStruct((nb, sq, qw), F32),
                   jax.ShapeDtypeStruct((nb, sq, NSA_GROUPS * SEL_LANES), F32)],
        compiler_params=_params(("parallel", "parallel")),
        name="nsa_cmp_select",
    )(q, kcvc)


def _block_mean_kernel(k_ref, o_ref, *, nblk):
    o_ref[...] = jnp.zeros(o_ref.shape, F32)
    for j in range(nblk):
        blk = k_ref[j * MOBA_BLOCK:(j + 1) * MOBA_BLOCK, :]
        o_ref[j:j + 1, :] = jnp.sum(blk, axis=0, keepdims=True) * (1.0 / MOBA_BLOCK)


def moba_block_mean(k):
    nb, s, w = k.shape
    nblk = s // MOBA_BLOCK
    return pl.pallas_call(
        functools.partial(_block_mean_kernel, nblk=nblk),
        grid=(nb,),
        in_specs=[pl.BlockSpec((None, s, w), lambda b: (b, 0, 0))],
        out_specs=pl.BlockSpec((None, SEL_LANES, w), lambda b: (b, 0, 0)),
        out_shape=jax.ShapeDtypeStruct((nb, SEL_LANES, w), F32),
        compiler_params=_params(("parallel",)),
        name="moba_block_mean",
    )(k)


def _page_sum_kernel(x_ref, o_ref, *, pp):
    ones = jnp.ones((SUBLANE, x_ref.shape[2]), MXU_DTYPE)
    f = lambda a: lax.dot_general(ones, a, _NT, preferred_element_type=F32)
    x = x_ref[...].reshape(pp * x_ref.shape[1], x_ref.shape[2])
    h1 = _mx(x)
    r1 = x - h1.astype(F32)
    h2 = _mx(r1)
    h3 = _mx(r1 - h2.astype(F32))
    o_ref[...] = (f(h1) + (f(h2) + f(h3)))[0:1]


def page_sum(pages, layer):
    _, npg, w, ps = pages.shape
    pp = _largest_divisor(npg, CMP_PAGES)
    return pl.pallas_call(
        functools.partial(_page_sum_kernel, pp=pp),
        grid=(npg // pp,),
        in_specs=[pl.BlockSpec((None, pp, w, ps), lambda i: (layer, i, 0, 0))],
        out_specs=pl.BlockSpec((None, 1, pp * w), lambda i: (i, 0, 0)),
        out_shape=jax.ShapeDtypeStruct((npg // pp, 1, pp * w), F32),
        compiler_params=_params(("parallel",)),
        name="page_sum",
    )(pages).reshape(npg, w)


def _kmean_gather_kernel(pt_ref, tbl_ref, o_ref, *, nblk, ppb):
    b = pl.program_id(0)
    o_ref[...] = jnp.zeros(o_ref.shape, F32)
    for j in range(nblk):
        acc = tbl_ref[pl.ds(pt_ref[b, j * ppb], 1), :]
        for k in range(1, ppb):
            acc = acc + tbl_ref[pl.ds(pt_ref[b, j * ppb + k], 1), :]
        o_ref[j:j + 1, :] = acc * (1.0 / MOBA_BLOCK)


def moba_kmean_paged(page_sums, page_table):
    nb, npg = page_table.shape
    ppb = MOBA_BLOCK // PAGE_SIZE
    nblk = npg // ppb
    w = page_sums.shape[1]
    grid_spec = pltpu.PrefetchScalarGridSpec(
        num_scalar_prefetch=1,
        grid=(nb,),
        in_specs=[pl.BlockSpec(page_sums.shape, lambda b, pt: (0, 0))],
        out_specs=pl.BlockSpec((None, SEL_LANES, w), lambda b, pt: (b, 0, 0)),
    )
    return pl.pallas_call(
        functools.partial(_kmean_gather_kernel, nblk=nblk, ppb=ppb),
        grid_spec=grid_spec,
        out_shape=jax.ShapeDtypeStruct((nb, SEL_LANES, w), F32),
        compiler_params=_params(("arbitrary",)),
        name="moba_kmean_paged",
    )(page_table, page_sums)


def _moba_select_kernel(q_ref, km_ref, sel_ref, *, tq, t0, n_full, n_sel, own_in_range):
    tbase = t0 + pl.program_id(1) * tq
    q = q_ref[...]
    km = km_ref[...]
    lane_q = _iota(q.shape, 1)
    lane = _iota((tq, SEL_LANES), 1)
    blk_q = (tbase + _iota((tq, 1), 0)) >> _LOG2_MOBA
    cand = (lane < blk_q) & (lane < n_full)
    for hd in range(MOBA_HEADS):
        qh = jnp.where((lane_q >> _LOG2_DH) == hd, q, 0.0)
        gs = _dot3_nt(qh, km)
        sel = _topk_mask(gs, cand, n_sel)
        if own_in_range:
            sel = jnp.where(lane == blk_q, 1.0, sel)
        sel_ref[:, hd * SEL_LANES:(hd + 1) * SEL_LANES] = sel


def moba_select(q, kmean, *, tq, t0, n_full, own_in_range):
    nb, sq, qw = q.shape
    kern = functools.partial(_moba_select_kernel, tq=tq, t0=t0, n_full=n_full,
                             n_sel=min(MOBA_TOPK, n_full), own_in_range=own_in_range)
    return pl.pallas_call(
        kern,
        grid=(nb, sq // tq),
        in_specs=[pl.BlockSpec((None, tq, qw), lambda b, i: (b, i, 0)),
                  pl.BlockSpec((None,) + kmean.shape[1:], lambda b, i: (b, 0, 0))],
        out_specs=pl.BlockSpec((None, tq, MOBA_HEADS * SEL_LANES), lambda b, i: (b, i, 0)),
        out_shape=jax.ShapeDtypeStruct((nb, sq, MOBA_HEADS * SEL_LANES), F32),
        compiler_params=_params(("parallel", "parallel")),
        name="moba_select",
    )(q, kmean)


def _nsa_jobs(slopes):
    heads = [(g, c) for g in range(NSA_GROUPS) for c in range(NSA_HPG)]
    return [dict(qcols=[(c * LANE, LANE) for g, c in heads],
                 qmasks=[(g * NSA_DH, (g + 1) * NSA_DH) for g, c in heads], q2cols=None,
                 kcol=(0, LANE), vcol=(0, LANE), slopes=[slopes[g][c] for g, c in heads],
                 sels=[g * SEL_LANES for g, c in heads], sel_shift=_LOG2_SEL,
                 outs=[(0, c * LANE, g * NSA_DH, (g + 1) * NSA_DH) for g, c in heads])]


def _moba_jobs(slopes):
    hs = range(MOBA_HEADS)
    width = MOBA_HEADS * MOBA_DH
    return [dict(qcols=[(0, width) for _ in hs], qmasks=[(h * MOBA_DH, (h + 1) * MOBA_DH) for h in hs],
                 q2cols=None, kcol=(0, width), vcol=(0, width), slopes=[slopes[h] for h in hs],
                 sels=[h * SEL_LANES for h in hs], sel_shift=_LOG2_MOBA,
                 outs=[((h // 2) * LANE, (h // 2) * LANE, (h % 2) * MOBA_DH, (h % 2 + 1) * MOBA_DH) for h in hs])]


def _stack_q(q_ref, job, tq):
    parts = []
    for (off, w), qmask in zip(job["qcols"], job["qmasks"]):
        x = q_ref[:, off:off + w]
        if qmask is not None:
            lo, hi = qmask
            ln = _iota(x.shape, 1)
            x = jnp.where((ln >= lo) & (ln < hi), x, 0.0)
        parts.append(x)
    qs = _mx(jnp.concatenate(parts, axis=0))
    q2 = None
    if job["q2cols"] is not None:
        q2 = _mx(jnp.concatenate([q_ref[:, off:off + w] for off, w in job["q2cols"]], axis=0))
    slope_col = None
    if job["slopes"] is not None:
        slope_col = jnp.concatenate([jnp.full((tq, 1), s, F32) for s in job["slopes"]], axis=0)
    return qs, q2, slope_col


def _block_mask(sel_ref, job, k0, span):
    shift = job["sel_shift"]
    expand = _mx((_iota((SEL_LANES, span), 0) == ((k0 + _iota((SEL_LANES, span), 1)) >> shift)).astype(F32))
    ems = {}
    for soff in job["sels"]:
        if soff not in ems:
            ems[soff] = jnp.dot(_mx(sel_ref[:, soff:soff + SEL_LANES]), expand, preferred_element_type=F32)
    return jnp.concatenate([ems[soff] for soff in job["sels"]], axis=0)


def _softmax_step(s, allowed, v, m, l, acc, v_t=False):
    if allowed is not None:
        s = jnp.where(allowed, s, NEG_INF)
    m_new = jnp.maximum(m, jnp.max(s, axis=-1, keepdims=True))
    m_safe = jnp.where(m_new > NEG_INF, m_new, 0.0)
    alpha = jnp.exp(m - m_safe)
    p = jnp.exp(s - m_safe)
    l = alpha * l + jnp.sum(p, axis=-1, keepdims=True)
    acc = alpha * acc + (_dot_nt(p, v) if v_t else _dot(p, v))
    return m_new, l, acc


def _write_outputs(o_ref, jobs, results, tq):
    lane = _iota((tq, LANE), 1)
    cols = {}
    for job, res in zip(jobs, results):
        for k, (voff, off, lo, hi) in enumerate(job["outs"]):
            piece = res[k * tq:(k + 1) * tq, voff:voff + LANE]
            if (lo, hi) == (0, LANE):
                cols[off] = piece
            else:
                prev = cols.get(off, jnp.zeros((tq, LANE), F32))
                cols[off] = jnp.where((lane >= lo) & (lane < hi), piece, prev)
    for off, val in cols.items():
        o_ref[:, off:off + LANE] = val


def _flash_kernel(*refs, jobs, tq, tk, q_pos0, window, has_sel, has_v):
    refs = list(refs)
    q_ref = refs.pop(0)
    k_ref = refs.pop(0)
    v_ref = refs.pop(0) if has_v else k_ref
    sel_ref = refs.pop(0) if has_sel else None
    o_ref = refs.pop(0)
    t_lo = q_pos0 + pl.program_id(1) * tq
    n_kt = k_ref.shape[0] // tk
    kt_hi = jnp.minimum((t_lo + tq - 1) // tk + 1, n_kt)
    kt_lo = jnp.maximum(t_lo - window + 1, 0) // tk if window else 0
    results = []
    for job in jobs:
        qs, q2, slope_col = _stack_q(q_ref, job, tq)
        nq = len(job["qcols"])
        r = nq * tq
        t_rows = t_lo + (_iota((r, 1), 0) & (tq - 1))
        koff, kw = job["kcol"]
        voff, vw = job["vcol"]
        masked = has_sel and job["sels"] is not None
        t_win = t_rows - window

        def body(kt, carry, causal, qs=qs, slope_col=slope_col, t_rows=t_rows, t_win=t_win, koff=koff, kw=kw,
                 voff=voff, vw=vw, masked=masked, job=job):
            m, l, acc = carry
            k0 = pl.multiple_of(kt * tk, tk)
            ktile = k_ref[pl.ds(k0, tk), koff:koff + kw]
            vtile = v_ref[pl.ds(k0, tk), voff:voff + vw]
            s = _dot_nt(qs, ktile)
            kpos = k0 + _iota((1, tk), 1)
            allowed = None
            if causal:
                allowed = t_rows >= kpos
                if window:
                    allowed = allowed & (t_win < kpos)
            if slope_col is not None:
                s = s - slope_col * (t_lo - kpos).astype(F32)
            if masked:
                picked = _block_mask(sel_ref, job, k0, tk) > 0.5
                allowed = picked if allowed is None else allowed & picked
            return _softmax_step(s, allowed, vtile, m, l, acc)

        init = (jnp.full((r, 1), NEG_INF, F32), jnp.zeros((r, 1), F32), jnp.zeros((r, vw), F32))
        if window:
            m, l, acc = lax.fori_loop(kt_lo, kt_hi, functools.partial(body, causal=True), init)
        else:
            kt_mid = jnp.clip((t_lo + 1) // tk, kt_lo, kt_hi)
            carry = lax.fori_loop(kt_lo, kt_mid, functools.partial(body, causal=False), init)
            m, l, acc = lax.fori_loop(kt_mid, kt_hi, functools.partial(body, causal=True), carry)
        results.append(acc / jnp.maximum(l, 1e-30))
    _write_outputs(o_ref, jobs, results, tq)


def flash_attention(q, k, v, sel, *, jobs, out_width, tq, tk, q_pos0=0, window=0):
    nb, sq, qw = q.shape
    sk = k.shape[1]
    ops = [q, k]
    specs = [pl.BlockSpec((None, tq, qw), lambda b, i: (b, i, 0)),
             pl.BlockSpec((None, sk, k.shape[2]), lambda b, i: (b, 0, 0))]
    if v is not None:
        ops.append(v)
        specs.append(pl.BlockSpec((None, sk, v.shape[2]), lambda b, i: (b, 0, 0)))
    if sel is not None:
        ops.append(sel)
        specs.append(pl.BlockSpec((None, tq, sel.shape[2]), lambda b, i: (b, i, 0)))
    kern = functools.partial(_flash_kernel, jobs=jobs, tq=tq, tk=tk, q_pos0=q_pos0, window=window,
                             has_sel=sel is not None, has_v=v is not None)
    return pl.pallas_call(
        kern,
        grid=(nb, sq // tq),
        in_specs=specs,
        out_specs=pl.BlockSpec((None, tq, out_width), lambda b, i: (b, i, 0)),
        out_shape=jax.ShapeDtypeStruct((nb, sq, out_width), F32),
        compiler_params=_params(("parallel", "parallel")),
        name="flash_attention",
    )(*ops)


def _paged_kernel(pt_ref, *refs, jobs, npages, page_len, past_len, key_pos0, window, n_new, has_sel, has_k2,
                  has_v, k_t, k2_t, v_t):
    refs = list(refs)
    q_ref = refs.pop(0)
    sel_ref = refs.pop(0) if has_sel else None
    knew_ref = refs.pop(0)
    k2new_ref = refs.pop(0) if has_k2 else None
    vnew_ref = refs.pop(0) if has_v else knew_ref
    k_refs = [refs.pop(0) for _ in range(npages)]
    k2_refs = [refs.pop(0) for _ in range(npages)] if has_k2 else None
    v_refs = [refs.pop(0) for _ in range(npages)] if has_v else k_refs
    o_ref = refs.pop(0)
    m_scr, l_scr, acc_scr = refs
    j = pl.program_id(1)
    tq = q_ref.shape[0]
    span = npages * page_len

    @pl.when(j == 0)
    def _():
        m_scr[...] = jnp.full(m_scr.shape, NEG_INF, F32)
        l_scr[...] = jnp.zeros(l_scr.shape, F32)
        acc_scr[...] = jnp.zeros(acc_scr.shape, F32)

    cat = lambda rs, t: jnp.concatenate([r[...] for r in rs], axis=1 if t else 0)
    kcat = cat(k_refs, k_t)
    vcat = cat(v_refs, v_t) if has_v else kcat
    k2cat = _mx(cat(k2_refs, k2_t)) if has_k2 else None
    k0 = j * span
    kpos = key_pos0 + k0 + _iota((1, span), 1)
    row0 = 0
    stacked = []
    for job in jobs:
        qs, q2, slope_col = _stack_q(q_ref, job, tq)
        nq = len(job["qcols"])
        r = nq * tq
        t_rows = past_len + (_iota((r, 1), 0) & (tq - 1))
        koff, kw = job["kcol"]
        voff, vw = job["vcol"]
        s = _dot(qs, kcat[koff:koff + kw, :]) if k_t else _dot_nt(qs, kcat[:, koff:koff + kw])
        if q2 is not None:
            s = s + (jnp.dot(q2, k2cat, preferred_element_type=F32) if k2_t
                     else lax.dot_general(q2, k2cat, _NT, preferred_element_type=F32))
        allowed = t_rows >= kpos
        if window:
            allowed = allowed & ((t_rows - window) < kpos)
        if slope_col is not None:
            s = s - slope_col * (past_len - kpos).astype(F32)
        if has_sel and job["sels"] is not None:
            allowed = allowed & (_block_mask(sel_ref, job, k0, span) > 0.5)
        rows = slice(row0, row0 + r)
        vsl = vcat[voff:voff + vw, :] if v_t else vcat[:, voff:voff + vw]
        m, l, acc = _softmax_step(s, allowed, vsl, m_scr[rows], l_scr[rows], acc_scr[rows], v_t)
        m_scr[rows] = m
        l_scr[rows] = l
        acc_scr[rows] = acc
        stacked.append((qs, q2, slope_col, t_rows, rows, r))
        row0 += r

    @pl.when(j == pl.num_programs(1) - 1)
    def _():
        results = []
        n_rows = knew_ref.shape[0]
        cpos = _iota((1, n_rows), 1)
        for job, (qs, q2, slope_col, t_rows, rows, r) in zip(jobs, stacked):
            koff, kw = job["kcol"]
            voff, vw = job["vcol"]
            s = _dot_nt(qs, knew_ref[:, koff:koff + kw])
            if q2 is not None:
                s = s + _dot_nt(q2, k2new_ref[...])
            allowed = (t_rows >= past_len + cpos) & (cpos < n_new)
            if slope_col is not None:
                s = s + slope_col * cpos.astype(F32)
            m, l, acc = _softmax_step(s, allowed, vnew_ref[:, voff:voff + vw],
                                      m_scr[rows], l_scr[rows], acc_scr[rows])
            results.append(acc / jnp.maximum(l, 1e-30))
        _write_outputs(o_ref, jobs, results, tq)


def paged_attention(q, sel, k_new, k2_new, v_new, k_pool, k2_pool, v_pool, page_table, layer, *,
                    jobs, out_width, past_len, n_new, k_t, k2_t=False, v_t=False, key_pos0=0, window=0):
    nb, tq, qw = q.shape
    npg = page_table.shape[1]
    p = _largest_divisor(npg, DEC_PAGES)
    page_len = k_pool.shape[3] if k_t else k_pool.shape[2]
    if v_pool is None:
        v_t = k_t
    seq = lambda a: pl.BlockSpec((None,) + a.shape[1:], lambda b, j, pt: (b, 0, 0))
    page = lambda pool, k: pl.BlockSpec((None, None) + pool.shape[2:],
                                        lambda b, j, pt: (layer, pt[b, j * p + k], 0, 0))
    ops, specs = [q], [seq(q)]
    if sel is not None:
        ops.append(sel)
        specs.append(seq(sel))
    ops.append(k_new)
    specs.append(seq(k_new))
    if k2_new is not None:
        ops.append(k2_new)
        specs.append(seq(k2_new))
    if v_new is not None:
        ops.append(v_new)
        specs.append(seq(v_new))
    for pool in (k_pool, k2_pool, v_pool):
        if pool is not None:
            ops += [pool] * p
            specs += [page(pool, k) for k in range(p)]
    rows = sum(len(job["qcols"]) for job in jobs) * tq
    acc_w = max(job["vcol"][1] for job in jobs)
    grid_spec = pltpu.PrefetchScalarGridSpec(
        num_scalar_prefetch=1,
        grid=(nb, npg // p),
        in_specs=specs,
        out_specs=pl.BlockSpec((None, tq, out_width), lambda b, j, pt: (b, 0, 0)),
        scratch_shapes=[pltpu.VMEM((rows, 1), F32), pltpu.VMEM((rows, 1), F32), pltpu.VMEM((rows, acc_w), F32)],
    )
    kern = functools.partial(_paged_kernel, jobs=jobs, npages=p, page_len=page_len, past_len=past_len,
                             key_pos0=key_pos0, window=window, n_new=n_new, has_sel=sel is not None,
                             has_k2=k2_pool is not None, has_v=v_pool is not None, k_t=k_t, k2_t=k2_t, v_t=v_t)
    return pl.pallas_call(
        kern,
        grid_spec=grid_spec,
        out_shape=jax.ShapeDtypeStruct((nb, tq, out_width), F32),
        compiler_params=_params(("parallel", "arbitrary")),
        name="paged_attention",
    )(page_table, *ops)


def _merge_kernel(x_ref, sh_ref, sc_ref, gt_ref, g_ref, oc_ref, os_ref, ow_ref, gate_ref, ctx_ref, om_ref,
                  wmg_ref, wa_ref, wuv_ref, wb_ref, wc_ref, wo_ref, o_ref):
    x = x_ref[...]
    h = _modulate(x, g_ref[...], sh_ref[...], sc_ref[...])
    mg = _sigmoid(_dot(h, wmg_ref[...]))
    n_chunk = NSA_HEADS
    width = n_chunk * NSA_DH
    rowi = _iota((LANE, N_BRANCH * width), 0)
    coli = _iota((LANE, N_BRANCH * width), 1)
    expand = _mx((rowi == (coli >> _LOG2_DH)).astype(F32))
    ge = _dot_exact01(gate_ref[...], expand)
    oa = ge[:, 0:width] * oc_ref[...] + ge[:, width:2 * width] * os_ref[...] + ge[:, 2 * width:] * ow_ref[...]
    ob = _dot(ctx_ref[...], wuv_ref[...])
    d = x.shape[1]
    mixed = (mg[:, 0:d] * _dot(oa, wa_ref[...]) + mg[:, d:2 * d] * _dot(ob, wb_ref[...])
             + mg[:, 2 * d:] * _dot(om_ref[...], wc_ref[...]))
    o_ref[...] = x + gt_ref[...] * _dot(mixed, wo_ref[...])


def merge(x, mod_ops, norm_g, oc, os_, ow, gate, ctx, om, lw, tm):
    t, d = x.shape
    full = lambda a: pl.BlockSpec(a.shape, lambda i: (0,) * a.ndim)
    row = lambda a: pl.BlockSpec((tm, a.shape[1]), lambda i: (i, 0))
    (sh, sh_s), (sc, sc_s), (gt, gt_s) = mod_ops[0], mod_ops[1], mod_ops[2]
    ws = [lw["w_mg"], lw["w_br_nsa"], lw["w_uv"], lw["w_br_mla"], lw["w_br_moba"], lw["w_out"]]
    acts = [oc, os_, ow, gate, ctx, om]
    return pl.pallas_call(
        _merge_kernel,
        grid=(t // tm,),
        in_specs=[row(x), sh_s, sc_s, gt_s, full(norm_g)] + [row(a) for a in acts] + [full(w) for w in ws],
        out_specs=row(x),
        out_shape=jax.ShapeDtypeStruct((t, d), F32),
        compiler_params=_params(("parallel",)),
        name="merge",
    )(x, sh, sc, gt, norm_g, *acts, *ws)


def _ffn_kernel(x_ref, sh_ref, sc_ref, gt_ref, g_ref, wi_ref, wo_ref, fn_ref, o_ref, *, final):
    x = x_ref[...]
    h = _modulate(x, g_ref[...], sh_ref[...], sc_ref[...])
    au = _dot(h, wi_ref[...])
    hid = au.shape[1] // 2
    y = x + gt_ref[...] * _dot(_silu(au[:, :hid]) * au[:, hid:], wo_ref[...])
    o_ref[...] = _rms(y, fn_ref[...]) if final else y


def ffn(x, mod_ops, norm_g, lw, final_norm, final, tm):
    t, d = x.shape
    full = lambda a: pl.BlockSpec(a.shape, lambda i: (0,) * a.ndim, pipeline_mode=pl.Buffered(1))
    row = pl.BlockSpec((tm, d), lambda i: (i, 0))
    (sh, sh_s), (sc, sc_s), (gt, gt_s) = mod_ops[3], mod_ops[4], mod_ops[5]
    return pl.pallas_call(
        functools.partial(_ffn_kernel, final=final),
        grid=(t // tm,),
        in_specs=[row, sh_s, sc_s, gt_s, full(norm_g), full(lw["ffn_in"]), full(lw["ffn_out"]), full(final_norm)],
        out_specs=row,
        out_shape=jax.ShapeDtypeStruct((t, d), F32),
        compiler_params=_params(("parallel",)),
        name="ffn",
    )(x, sh, sc, gt, norm_g, lw["ffn_in"], lw["ffn_out"], final_norm)


def _prep_layer(l, w_in, mla_q_norm, mla_w_uq, mla_kv_norm, mla_w_uk, mla_w_uv, nsa_cmp_w1, nsa_cmp_pe,
                nsa_cmp_w2, w_br_nsa, w_br_mla, w_br_moba, w_out, ffn_w_in, ffn_w_out):
    d = w_in.shape[1]
    g, hpg, dh = NSA_GROUPS, NSA_HPG, NSA_DH
    cuts = np.cumsum(IN_WIDTHS)[:-1].tolist()
    nq, nkv, ng, mql, mkvl, mkpe, mqkv, mg = jnp.split(w_in[l], cuts, axis=1)
    nq_p = nq.reshape(d, g, hpg, dh).transpose(0, 2, 1, 3).reshape(d, g * hpg * dh)
    ng_p = ng.reshape(d, g, hpg, N_BRANCH).transpose(0, 3, 2, 1).reshape(d, N_BRANCH * g * hpg)
    ng_p = jnp.pad(ng_p, ((0, 0), (0, LANE - ng_p.shape[1])))
    half = MLA_ROPE // 2
    swap = lambda a: jnp.concatenate([a[..., half:], a[..., :half]], axis=-1)
    rep = LANE // MLA_ROPE
    f1 = jnp.tile(mkpe, (1, rep))
    f2 = jnp.tile(swap(mkpe), (1, rep))
    w_a = jnp.concatenate([nq_p, nkv, ng_p, mql, mkvl, f1, f2, mqkv], axis=1)
    uq = mla_w_uq[l]
    uq_nope = uq[:, :, :MLA_NOPE].reshape(MLA_Q_LORA, MLA_HEADS * MLA_NOPE)
    uq_pe = uq[:, :, MLA_NOPE:]
    w_uq = jnp.concatenate([uq_nope, uq_pe.reshape(MLA_Q_LORA, -1), swap(uq_pe).reshape(MLA_Q_LORA, -1)], axis=1)
    uk = mla_w_uk[l]
    w_uk = block_diag(*[uk[:, hd, :].T for hd in range(MLA_HEADS)])
    uv = mla_w_uv[l]
    w_uv = block_diag(*[uv[:, hd, :] for hd in range(MLA_HEADS)])
    w1 = nsa_cmp_w1[l]
    hid = w1.shape[-1]
    wc = jnp.zeros((NSA_CMP_STRIDE, 2, g, dh, NSA_CMP_R, 2, g, hid), F32)
    for kv in range(2):
        for gi in range(g):
            wc = wc.at[:, kv, gi, :, :, kv, gi, :].set(w1[kv].transpose(1, 2, 0, 3))
    w_chunks = wc.reshape(NSA_CMP_STRIDE, 2 * g * dh, NSA_CMP_R * 2 * g * hid)
    pe = nsa_cmp_pe[l].reshape(2, NSA_CMP_R, NSA_CMP_STRIDE, dh)
    pe_h = jnp.einsum('krcd,krcdh->kh', pe, w1, precision=lax.Precision.HIGHEST)
    pe_const = jnp.concatenate([pe_h[kv] for kv in range(2) for _ in range(g)])[None, :]
    w2 = nsa_cmp_w2[l]
    w2_bd = block_diag(*[w2[kv] for kv in range(2) for _ in range(g)])
    br_nsa = w_br_nsa[l].reshape(g, hpg, dh, d).transpose(1, 0, 2, 3).reshape(g * hpg * dh, d)
    return dict(
        w_a=_mx(w_a), qn=mla_q_norm[l][None, :], w_uq=_mx(w_uq), w_uk=_mx(w_uk), kvn=mla_kv_norm[l][None, :],
        w_chunks=_mx(w_chunks), pe_const=pe_const, w2_bd=_mx(w2_bd), w_mg=_mx(mg), w_br_nsa=_mx(br_nsa),
        w_uv=_mx(w_uv), w_br_mla=_mx(w_br_mla[l]), w_br_moba=_mx(w_br_moba[l]), w_out=_mx(w_out[l]),
        ffn_in=_mx(ffn_w_in[l]), ffn_out=_mx(ffn_w_out[l]))


def _rope_tables(pos):
    half = MLA_ROPE // 2
    inv = ROPE_THETA ** (-jnp.arange(half, dtype=F32) / half)
    ang = pos.astype(F32)[:, None] * inv
    cos, sin = jnp.cos(ang), jnp.sin(ang)
    rep = LANE // MLA_ROPE
    return (jnp.tile(jnp.concatenate([cos, cos], axis=1), (1, rep)),
            jnp.tile(jnp.concatenate([-sin, sin], axis=1), (1, rep)))


def _pad_rows(a, rows):
    return jnp.pad(a, ((0, 0), (0, rows - a.shape[1]), (0, 0)))


def kernel(x_prompt, x_sample, c_prompt, c_sample, cache_nsa_cmp_kv, cache_nsa_slc_k, cache_nsa_slc_v, state_nsa_win_kv, cache_mla_latent, cache_mla_kpe, cache_moba_k, cache_moba_v, page_table, ada_w, ada_b, norm_mix, norm_ffn, w_in, nsa_cmp_w1, nsa_cmp_pe, nsa_cmp_w2, mla_q_norm, mla_w_uq, mla_kv_norm, mla_w_uk, mla_w_uv, w_br_nsa, w_br_mla, w_br_moba, w_out, ffn_w_in, ffn_w_out, final_norm):
    depth = w_in.shape[0]
    nb, seq, d = x_prompt.shape
    db, ds, _ = x_sample.shape
    n_pool = cache_nsa_slc_k.shape[1]
    npg = page_table.shape[1]
    past_len = npg * PAGE_SIZE
    win_len = state_nsa_win_kv.shape[2]
    tp, ts = nb * seq, db * ds
    tm = TOKEN_TILE
    tm_s = min(TOKEN_TILE, ts)
    assert seq % MOBA_BLOCK == 0 and seq % ATT_TK_LONG == 0 and seq % (2 * ATT_TQ) == 0 and seq % tm == 0 and ts % tm_s == 0 and tm_s % SUBLANE == 0
    assert past_len % MOBA_BLOCK == 0 and ds <= DEC_Q and win_len == NSA_WINDOW <= past_len
    assert past_len // NSA_SEL_BLOCK <= SEL_LANES and seq // NSA_SEL_BLOCK <= SEL_LANES
    nsa_sl, moba_sl = alibi_slopes()
    nsa_jobs, moba_jobs = _nsa_jobs(nsa_sl), _moba_jobs(moba_sl)
    win_jobs = [dict(j, sels=None, vcol=(LANE, LANE)) for j in nsa_jobs]
    mla_jobs = [dict(qcols=[(hd * 256, 256) for hd in range(MLA_HEADS)], qmasks=[None] * MLA_HEADS, q2cols=None,
                     kcol=(0, 256), vcol=(0, LANE), slopes=None, sels=None, sel_shift=None,
                     outs=[(0, hd * LANE, 0, LANE) for hd in range(MLA_HEADS)])]
    mla_dec_jobs = [dict(mla_jobs[0], qcols=[(hd * 256, LANE) for hd in range(MLA_HEADS)], kcol=(0, LANE),
                         q2cols=[(hd * 256 + LANE + hd * MLA_ROPE, MLA_ROPE) for hd in range(MLA_HEADS)])]

    n_c = nb + db
    c_all = jnp.pad(jnp.concatenate([c_prompt, c_sample], axis=0), ((0, -n_c % SUBLANE), (0, 0)))
    mod_all = adaln_all(c_all, ada_w, ada_b)

    def mod_ops_prompt(l):
        arr = mod_all[l, :nb].reshape(nb * N_MOD, 1, d)
        tiles = seq // tm
        return [(arr, pl.BlockSpec((None, 1, d), lambda i, k=k: ((i // tiles) * N_MOD + k, 0, 0)))
                for k in range(N_MOD)]

    def mod_ops_sample(l):
        arr = jnp.repeat(mod_all[l, nb:n_c].reshape(db, N_MOD, d), ds, axis=0).transpose(1, 0, 2)
        return [(arr, pl.BlockSpec((None, tm_s, d), lambda i, k=k: (k, i, 0))) for k in range(N_MOD)]

    c4p, s4p = _rope_tables(jnp.arange(seq))
    c4s, s4s = _rope_tables(jnp.tile(past_len + jnp.arange(ds), db))
    tiles_p = seq // tm
    tbl_spec_p = pl.BlockSpec((tm, LANE), lambda i: (i % tiles_p, 0))
    tbl_spec_s = pl.BlockSpec((tm_s, LANE), lambda i: (i, 0))

    def pages_t(a):
        nd = a.ndim
        return jnp.transpose(a, (0, 1) + tuple(range(3, nd)) + (2,)).reshape(a.shape[0], a.shape[1], -1, a.shape[2])

    cmp_pool = pages_t(cache_nsa_cmp_kv)
    slc_k_pool, slc_v_pool = pages_t(cache_nsa_slc_k), pages_t(cache_nsa_slc_v)
    moba_k_pool, moba_v_pool = pages_t(cache_moba_k), pages_t(cache_moba_v)
    kpe_pool = pages_t(cache_mla_kpe)
    win_pool = pages_t(state_nsa_win_kv)
    ident_win = jnp.arange(db, dtype=jnp.int32).reshape(db, 1)
    ident_pt = jnp.arange(nb * (seq // PAGE_SIZE), dtype=jnp.int32).reshape(nb, seq // PAGE_SIZE)
    cpp = PAGE_SIZE // NSA_CMP_STRIDE

    xp = x_prompt.reshape(tp, d)
    xs = x_sample.reshape(ts, d)
    st_p, st_s = [], []
    for l in range(depth):
        lw = _prep_layer(l, w_in, mla_q_norm, mla_w_uq, mla_kv_norm, mla_w_uk, mla_w_uv, nsa_cmp_w1, nsa_cmp_pe,
                         nsa_cmp_w2, w_br_nsa, w_br_mla, w_br_moba, w_out, ffn_w_in, ffn_w_out)
        g_mix, g_ffn = norm_mix[l][None, :], norm_ffn[l][None, :]
        last = l == depth - 1
        fin = final_norm[None, :]

        mods = mod_ops_prompt(l)
        (q, cmp, sk, sv, win, gate, qmla, latkpe, mq, mk, mv) = inproj(xp, mods, g_mix, lw, c4p, s4p, tbl_spec_p, tm)
        b3 = lambda a: a.reshape(nb, seq, a.shape[-1])
        a_chunks = cmp_chunk_proj(cmp.reshape(1, tp // PAGE_SIZE, PAGE_SIZE, cmp.shape[-1]), 0, lw["w_chunks"])
        kcvc = cmp_assemble(a_chunks.reshape(tp // PAGE_SIZE, cpp, -1), ident_pt, lw["pe_const"], lw["w2_bd"])
        oc, sel = nsa_cmp_select(b3(q), kcvc, bb=1, tq=ATT_TQ, t0=0, nc=seq // NSA_CMP_STRIDE - 1,
                                 ns=seq // NSA_SEL_BLOCK, own_in_range=True, slopes=nsa_sl)
        os_ = flash_attention(b3(q), b3(sk), b3(sv), sel, jobs=nsa_jobs, out_width=512, tq=ATT_TQ, tk=ATT_TK_LONG)
        ow = flash_attention(b3(q), b3(win), None, None, jobs=win_jobs, out_width=512, tq=ATT_TQ, tk=ATT_TK,
                             window=NSA_WINDOW)
        ctx = flash_attention(b3(qmla), b3(latkpe), None, None, jobs=mla_jobs, out_width=512, tq=2 * ATT_TQ,
                              tk=ATT_TK_LONG)
        msel = moba_select(b3(mq), moba_block_mean(b3(mk)), tq=ATT_TQ, t0=0, n_full=seq // MOBA_BLOCK,
                           own_in_range=True)
        om = flash_attention(b3(mq), b3(mk), b3(mv), msel, jobs=moba_jobs, out_width=256, tq=2 * ATT_TQ,
                             tk=ATT_TK_LONG)
        f2 = lambda a: a.reshape(tp, a.shape[-1])
        x1 = merge(xp, mods, g_mix, f2(oc), f2(os_), f2(ow), gate, f2(ctx), f2(om), lw, tm)
        xp = ffn(x1, mods, g_ffn, lw, fin, last, tm)
        win_keep = min(NSA_WINDOW, seq)
        st_p.append((cmp.reshape(nb, seq, 2, NSA_GROUPS, NSA_DH), sk.reshape(nb, seq, NSA_GROUPS, NSA_DH),
                     sv.reshape(nb, seq, NSA_GROUPS, NSA_DH),
                     b3(win)[:, seq - win_keep:].reshape(nb, win_keep, 2, NSA_GROUPS, NSA_DH),
                     b3(latkpe)[:, :, :MLA_KV_LORA], b3(latkpe)[:, :, LANE:LANE + MLA_ROPE],
                     mk.reshape(nb, seq, MOBA_HEADS, MOBA_DH), mv.reshape(nb, seq, MOBA_HEADS, MOBA_DH)))

        mods = mod_ops_sample(l)
        (q, cmp, sk, sv, win, gate, qmla, latkpe, mq, mk, mv) = inproj(xs, mods, g_mix, lw, c4s, s4s, tbl_spec_s, tm_s)
        s3 = lambda a: a.reshape(db, ds, a.shape[-1])
        q8, qmla8, mq8 = (_pad_rows(s3(a), DEC_Q) for a in (q, qmla, mq))
        new16 = lambda a: _pad_rows(s3(a), DEC_NEW)
        a_pool = cmp_chunk_proj_t(cmp_pool, l, lw["w_chunks"])
        kcvc = cmp_assemble(a_pool.reshape(n_pool, cpp, -1), page_table, lw["pe_const"], lw["w2_bd"])
        oc8, sel8 = nsa_cmp_select(q8, kcvc, bb=_largest_divisor(db, DEC_SEQS), tq=DEC_Q, t0=past_len,
                                   nc=past_len // NSA_CMP_STRIDE - 1,
                                   ns=past_len // NSA_SEL_BLOCK, own_in_range=False, slopes=nsa_sl)
        os8 = paged_attention(q8, sel8, new16(sk), None, new16(sv), slc_k_pool, None, slc_v_pool, page_table, l,
                              jobs=nsa_jobs, out_width=512, past_len=past_len, n_new=ds, k_t=True, v_t=True)
        ow8 = paged_attention(q8, None, new16(win), None, None, win_pool, None, None, ident_win, l,
                              jobs=win_jobs, out_width=512, past_len=past_len, n_new=ds, k_t=True,
                              key_pos0=past_len - win_len, window=NSA_WINDOW)
        lat_new = new16(latkpe)
        ctx8 = paged_attention(qmla8, None, lat_new[:, :, :LANE], lat_new[:, :, LANE:LANE + MLA_ROPE], None,
                               cache_mla_latent, kpe_pool, None, page_table, l,
                               jobs=mla_dec_jobs, out_width=512, past_len=past_len, n_new=ds, k_t=False, k2_t=True)
        kmean = moba_kmean_paged(page_sum(moba_k_pool, l), page_table)
        msel8 = moba_select(mq8, kmean, tq=DEC_Q, t0=past_len, n_full=past_len // MOBA_BLOCK, own_in_range=False)
        om8 = paged_attention(mq8, msel8, new16(mk), None, new16(mv), moba_k_pool, None, moba_v_pool, page_table, l,
                              jobs=moba_jobs, out_width=256, past_len=past_len, n_new=ds, k_t=True, v_t=True)
        win_buf = state_nsa_win_kv[l].reshape(db, win_len, 2 * NSA_GROUPS * NSA_DH)
        win_cat = jnp.concatenate([win_buf, s3(win)], axis=1)
        d2 = lambda a: a[:, :ds].reshape(ts, a.shape[-1])
        x1 = merge(xs, mods, g_mix, d2(oc8), d2(os8), d2(ow8), gate, d2(ctx8), d2(om8), lw, tm_s)
        xs = ffn(x1, mods, g_ffn, lw, fin, last, tm_s)
        keep = min(NSA_WINDOW, past_len + ds)
        new_win = win_cat[:, -keep:].reshape(db, keep, 2, NSA_GROUPS, NSA_DH)
        st_s.append((cmp.reshape(db, ds, 2, NSA_GROUPS, NSA_DH), sk.reshape(db, ds, NSA_GROUPS, NSA_DH),
                     sv.reshape(db, ds, NSA_GROUPS, NSA_DH), new_win,
                     s3(latkpe)[:, :, :MLA_KV_LORA], s3(latkpe)[:, :, LANE:LANE + MLA_ROPE],
                     mk.reshape(db, ds, MOBA_HEADS, MOBA_DH), mv.reshape(db, ds, MOBA_HEADS, MOBA_DH)))

    outs_p = [jnp.stack(a, axis=0) for a in zip(*st_p)]
    outs_s = [jnp.stack(a, axis=0) for a in zip(*st_s)]
    return (xp.reshape(nb, seq, d), xs.reshape(db, ds, d), *outs_p, *outs_s)
```

```python
import functools

import jax
import jax.numpy as jnp
from jax import lax
import numpy as np
from jax.experimental import pallas as pl
from jax.experimental.pallas import tpu as pltpu
from jax.scipy.linalg import block_diag

D_MODEL = 1024
PAGE_SIZE = 128
NSA_HEADS = 8
NSA_GROUPS = 2
NSA_HPG = NSA_HEADS // NSA_GROUPS
NSA_DH = 64
NSA_CMP_LEN = 32
NSA_CMP_STRIDE = 16
NSA_CMP_R = NSA_CMP_LEN // NSA_CMP_STRIDE
NSA_CMP_HID = 2 * NSA_DH
NSA_SEL_BLOCK = 64
NSA_TOPN = 16
NSA_WINDOW = 512
MLA_HEADS = 4
MLA_Q_LORA = 256
MLA_KV_LORA = 128
MLA_NOPE = 64
MLA_ROPE = 32
MLA_V = 64
MLA_SCALE = (MLA_NOPE + MLA_ROPE) ** -0.5
ROPE_THETA = 10000.0
MOBA_HEADS = 4
MOBA_DH = 64
MOBA_BLOCK = 256
MOBA_TOPK = 3
FFN_HIDDEN = ((8 * D_MODEL + 3 * 256 - 1) // (3 * 256)) * 256
N_BRANCH = 3
N_MOD = 6
N_ALIBI = NSA_HEADS + MOBA_HEADS
RMS_EPS = 1e-6
IN_WIDTHS = (NSA_HEADS * NSA_DH, N_BRANCH * 2 * NSA_GROUPS * NSA_DH, N_BRANCH * NSA_HEADS,
             MLA_Q_LORA, MLA_KV_LORA, MLA_ROPE, 3 * MOBA_HEADS * MOBA_DH, N_BRANCH * D_MODEL)
NSA_SCALE = NSA_DH ** -0.5
MOBA_SCALE = MOBA_DH ** -0.5
_LOG2_DH = 6
_LOG2_SEL = 6
_LOG2_MOBA = 8
_LOG2_ROPE = 5
assert (1 << _LOG2_DH == NSA_DH == MOBA_DH and 1 << _LOG2_SEL == NSA_SEL_BLOCK
        and 1 << _LOG2_MOBA == MOBA_BLOCK and 1 << _LOG2_ROPE == MLA_ROPE)

F32 = jnp.float32
MXU_DTYPE = jnp.bfloat16
LANE = 128
SUBLANE = 8
VMEM_LIMIT = 56 * 1024 * 1024
TOKEN_TILE = 256
ATT_TQ = 128
ATT_TK = 256
ATT_TK_LONG = 512
DEC_Q = 8
DEC_NEW = 16
DEC_PAGES = 32
DEC_PAGES_SMALL = 64
DEC_SEQS = 8
CMP_PAGES = 32
SEL_LANES = LANE
NEG_INF = float("-inf")


def _params(sem):
    return pltpu.CompilerParams(dimension_semantics=sem, vmem_limit_bytes=VMEM_LIMIT)


def _mx(x):
    return x.astype(MXU_DTYPE)


def _dot(a, b):
    return jnp.dot(_mx(a), _mx(b), preferred_element_type=F32)


_NT = (((1,), (1,)), ((), ()))


def _dot_nt(a, b):
    return lax.dot_general(_mx(a), _mx(b), _NT, preferred_element_type=F32)


def _split2(x):
    hi = _mx(x)
    lo = _mx(x - hi.astype(F32))
    return hi, lo


def _dot3_nt(a, b):
    ah, al = _split2(a)
    bh, bl = _split2(b)
    f = lambda x, y: lax.dot_general(x, y, _NT, preferred_element_type=F32)
    return f(ah, bh) + (f(ah, bl) + f(al, bh))


def _dot_exact01(x, e):
    h1 = _mx(x)
    r1 = x - h1.astype(F32)
    h2 = _mx(r1)
    h3 = _mx(r1 - h2.astype(F32))
    f = lambda a: jnp.dot(a, e, preferred_element_type=F32)
    return f(h1) + (f(h2) + f(h3))


def _sigmoid(x):
    return 1.0 / (1.0 + jnp.exp(-x))


def _silu(x):
    return x * _sigmoid(x)


def _rms(x, g):
    return x * lax.rsqrt(jnp.mean(x * x, axis=-1, keepdims=True) + RMS_EPS) * g


def _modulate(x, g, shift, scale):
    return _rms(x, g) * (1.0 + scale) + shift


def _iota(shape, dim):
    return lax.broadcasted_iota(jnp.int32, shape, dim)


def _topk_mask(vals, cand, k):
    lane = _iota(vals.shape, 1).astype(F32)
    cur = jnp.where(cand, vals, NEG_INF)
    sel = jnp.zeros(vals.shape, F32)
    for _ in range(k):
        mx = jnp.max(cur, axis=-1, keepdims=True)
        hit = (cur == mx) & (cur > NEG_INF)
        idx = jnp.min(jnp.where(hit, lane, 1e9), axis=-1, keepdims=True)
        one = lane == idx
        sel = jnp.where(one, 1.0, sel)
        cur = jnp.where(one, NEG_INF, cur)
    return sel


def alibi_slopes():
    s = 2.0 ** (-8.0 * np.arange(1, N_ALIBI + 1) / N_ALIBI)
    step = N_ALIBI // MOBA_HEADS
    moba_idx = np.arange(MOBA_HEADS) * step + step - 1
    nsa_idx = np.setdiff1d(np.arange(N_ALIBI), moba_idx)
    nsa = np.asarray(s[nsa_idx], np.float32).reshape(NSA_GROUPS, NSA_HPG)
    moba = np.asarray(s[moba_idx], np.float32)
    return [[float(v) for v in row] for row in nsa], [float(v) for v in moba]


def _adaln_kernel(c_ref, w_ref, b_ref, o_ref):
    o_ref[...] = _dot(_silu(c_ref[...]), w_ref[...]) + b_ref[...]


def adaln_all(c_all, ada_w, ada_b):
    depth, d, n = ada_w.shape
    r = c_all.shape[0]
    tn = 1024
    return pl.pallas_call(
        _adaln_kernel,
        grid=(depth, n // tn),
        in_specs=[pl.BlockSpec((r, d), lambda l, j: (0, 0)),
                  pl.BlockSpec((None, d, tn), lambda l, j: (l, 0, j)),
                  pl.BlockSpec((None, 1, tn), lambda l, j: (l, 0, j))],
        out_specs=pl.BlockSpec((None, r, tn), lambda l, j: (l, 0, j)),
        out_shape=jax.ShapeDtypeStruct((depth, r, n), F32),
        compiler_params=_params(("parallel", "parallel")),
        name="adaln",
    )(c_all, _mx(ada_w), ada_b.reshape(depth, 1, n))


_C_Q, _C_CMP, _C_SK, _C_SV, _C_WIN, _C_GATE, _C_MQL, _C_MKV, _C_F1, _C_F2, _C_MOBA, _C_END = (
    0, 512, 768, 896, 1024, 1280, 1408, 1664, 1792, 1920, 2048, 2816)


def _inproj_kernel(x_ref, sh_ref, sc_ref, g_ref, wa_ref, qn_ref, wuq_ref, wuk_ref, kvn_ref, c4_ref, s4_ref,
                   oq, ocmp, osk, osv, owin, ogate, oqmla, olatkpe, omq, omk, omv):
    h = _modulate(x_ref[...], g_ref[...], sh_ref[...], sc_ref[...])
    y = _dot(h, wa_ref[...])
    oq[...] = y[:, _C_Q:_C_CMP] * NSA_SCALE
    ocmp[...] = y[:, _C_CMP:_C_SK]
    osk[...] = y[:, _C_SK:_C_SV]
    osv[...] = y[:, _C_SV:_C_WIN]
    owin[...] = y[:, _C_WIN:_C_GATE]
    ogate[...] = _sigmoid(y[:, _C_GATE:_C_MQL])
    c4 = c4_ref[...]
    s4 = s4_ref[...]
    olatkpe[:, 0:LANE] = _rms(y[:, _C_MKV:_C_F1], kvn_ref[...])
    olatkpe[:, LANE:2 * LANE] = y[:, _C_F1:_C_F2] * c4 + y[:, _C_F2:_C_MOBA] * s4
    mq = _dot(_rms(y[:, _C_MQL:_C_MKV], qn_ref[...]), wuq_ref[...])
    qpe = (mq[:, 256:384] * c4 + mq[:, 384:512] * s4) * MLA_SCALE
    qlat = _dot(mq[:, 0:256], wuk_ref[...]) * MLA_SCALE
    head_of_lane = _iota(qpe.shape, 1) >> _LOG2_ROPE
    for hd in range(MLA_HEADS):
        oqmla[:, hd * 256:hd * 256 + LANE] = qlat[:, hd * LANE:(hd + 1) * LANE]
        oqmla[:, hd * 256 + LANE:(hd + 1) * 256] = jnp.where(head_of_lane == hd, qpe, 0.0)
    omq[...] = y[:, _C_MOBA:_C_MOBA + 256] * MOBA_SCALE
    omk[...] = y[:, _C_MOBA + 256:_C_MOBA + 512]
    omv[...] = y[:, _C_MOBA + 512:_C_END]


_INPROJ_WIDTHS = (512, 256, 128, 128, 256, 128, 1024, 256, 256, 256, 256)


def inproj(x, mod_ops, norm_g, lw, c4, s4, table_spec, tm):
    t, d = x.shape
    (sh, sh_spec), (sc, sc_spec) = mod_ops[0], mod_ops[1]
    full = lambda a: pl.BlockSpec(a.shape, lambda i: (0,) * a.ndim)
    row = lambda w: pl.BlockSpec((tm, w), lambda i: (i, 0))
    return pl.pallas_call(
        _inproj_kernel,
        grid=(t // tm,),
        in_specs=[row(d), sh_spec, sc_spec, full(norm_g), full(lw["w_a"]), full(lw["qn"]), full(lw["w_uq"]),
                  full(lw["w_uk"]), full(lw["kvn"]), table_spec, table_spec],
        out_specs=[row(w) for w in _INPROJ_WIDTHS],
        out_shape=[jax.ShapeDtypeStruct((t, w), F32) for w in _INPROJ_WIDTHS],
        compiler_params=_params(("parallel",)),
        name="inproj",
    )(x, sh, sc, norm_g, lw["w_a"], lw["qn"], lw["w_uq"], lw["w_uk"], lw["kvn"], c4, s4)


def _cmp_a_core(x_refs, w_ref, o_ref, pp):
    cpp = PAGE_SIZE // NSA_CMP_STRIDE
    halves = []
    for kv, x_ref in enumerate(x_refs):
        acc = None
        for cp in range(NSA_CMP_STRIDE // 2):
            lhs = jnp.concatenate(
                [x_ref[:, pl.ds(2 * cp + k, cpp, stride=NSA_CMP_STRIDE), :].reshape(pp * cpp, LANE)
                 for k in range(2)], axis=1)
            part = _dot(lhs, w_ref[kv, cp])
            acc = part if acc is None else acc + part
        halves.append(acc)
    o_ref[...] = jnp.concatenate(halves, axis=1)


def _cmp_a_kernel(xa_ref, xb_ref, w_ref, o_ref, *, pp):
    _cmp_a_core((xa_ref, xb_ref), w_ref, o_ref, pp)


def _cmp_a_t_kernel(pt_ref, *refs, pp):
    x_refs = refs[:pp]
    w_ref, o_ref, xa_scr, xb_scr = refs[pp:]
    for p in range(pp):
        x = x_refs[p][...].T
        xa_scr[p] = x[:, 0:LANE]
        xb_scr[p] = x[:, LANE:2 * LANE]
    _cmp_a_core((xa_scr, xb_scr), w_ref, o_ref, pp)


def cmp_chunk_proj_t(pages_t, layer, page_list, w_chunks):
    _, _, width, ps = pages_t.shape
    n_used = page_list.shape[0]
    cpp = ps // NSA_CMP_STRIDE
    pp = _largest_divisor(n_used, CMP_PAGES)
    nout = 2 * w_chunks.shape[-1]
    page = lambda k: pl.BlockSpec((None, None, width, ps), lambda i, pt: (layer, pt[i * pp + k], 0, 0))
    grid_spec = pltpu.PrefetchScalarGridSpec(
        num_scalar_prefetch=1,
        grid=(n_used // pp,),
        in_specs=[page(k) for k in range(pp)] + [pl.BlockSpec(w_chunks.shape, lambda i, pt: (0, 0, 0, 0))],
        out_specs=pl.BlockSpec((pp * cpp, nout), lambda i, pt: (i, 0)),
        scratch_shapes=[pltpu.VMEM((pp, ps, LANE), F32), pltpu.VMEM((pp, ps, LANE), F32)],
    )
    return pl.pallas_call(
        functools.partial(_cmp_a_t_kernel, pp=pp),
        grid_spec=grid_spec,
        out_shape=jax.ShapeDtypeStruct((n_used * cpp, nout), F32),
        compiler_params=_params(("parallel",)),
        name="cmp_chunk_proj_t",
    )(page_list, *([pages_t] * pp), w_chunks)


def _largest_divisor(n, cap):
    for p in range(min(cap, n), 0, -1):
        if n % p == 0:
            return p
    return 1


def cmp_chunk_proj(pages, layer, w_chunks):
    _, npg, ps, width = pages.shape
    cpp = ps // NSA_CMP_STRIDE
    pp = _largest_divisor(npg, CMP_PAGES)
    nout = 2 * w_chunks.shape[-1]
    return pl.pallas_call(
        functools.partial(_cmp_a_kernel, pp=pp),
        grid=(npg // pp,),
        in_specs=[pl.BlockSpec((None, pp, ps, LANE), lambda i: (layer, i, 0, 0)),
                  pl.BlockSpec((None, pp, ps, LANE), lambda i: (layer, i, 0, 1)),
                  pl.BlockSpec(w_chunks.shape, lambda i: (0, 0, 0, 0))],
        out_specs=pl.BlockSpec((pp * cpp, nout), lambda i: (i, 0)),
        out_shape=jax.ShapeDtypeStruct((npg * cpp, nout), F32),
        compiler_params=_params(("parallel",)),
        name="cmp_chunk_proj",
    )(pages, pages, w_chunks)


def _cmp_asm_kernel(a_ref, pe_ref, w2_ref, o_ref):
    a = a_ref[...]
    n = a.shape[0]
    hid_w = NSA_CMP_HID
    first = jnp.concatenate([a[:, k * 2 * hid_w:k * 2 * hid_w + hid_w] for k in range(2 * NSA_GROUPS)], axis=1)
    second = jnp.concatenate([a[:, k * 2 * hid_w + hid_w:(k + 1) * 2 * hid_w] for k in range(2 * NSA_GROUPS)], axis=1)
    hid = first + pltpu.roll(second, n - 1, 0) + pe_ref[...]
    o_ref[...] = _dot(_silu(hid), w2_ref[...])


def cmp_assemble(a_seq, pe_const, w2_bd):
    nb, n, wa = a_seq.shape
    return pl.pallas_call(
        _cmp_asm_kernel,
        grid=(nb,),
        in_specs=[pl.BlockSpec((None, n, wa), lambda b: (b, 0, 0)),
                  pl.BlockSpec(pe_const.shape, lambda b: (0, 0)),
                  pl.BlockSpec(w2_bd.shape, lambda b: (0, 0))],
        out_specs=pl.BlockSpec((None, n, w2_bd.shape[1]), lambda b: (b, 0, 0)),
        out_shape=jax.ShapeDtypeStruct((nb, n, w2_bd.shape[1]), F32),
        compiler_params=_params(("parallel",)),
        name="cmp_assemble",
    )(a_seq, pe_const, w2_bd)


def _nsa_cmp_kernel(q_ref, kcvc_ref, oc_ref, sel_ref, *, bb, tq, t0, nc, ns, n_top, own_in_range, slopes):
    tbase = t0 + pl.program_id(1) * tq
    ncp = kcvc_ref.shape[1]
    r = NSA_GROUPS * NSA_HPG * tq
    lane = _iota((tq, LANE), 1)
    t_rows = tbase + (_iota((r, 1), 0) & (tq - 1))
    cidx = _iota((1, ncp), 1)
    d_c = t_rows - (cidx * NSA_CMP_STRIDE + (NSA_CMP_LEN - 1))
    valid = (d_c >= 0) & (cidx < nc)
    d_cf = d_c.astype(F32)
    ci = _iota((ncp, SEL_LANES), 0) * NSA_CMP_STRIDE
    sj = _iota((ncp, SEL_LANES), 1) * NSA_SEL_BLOCK
    cover = ((ci < sj + NSA_SEL_BLOCK) & (ci + NSA_CMP_LEN > sj)
             & (_iota((ncp, SEL_LANES), 0) < nc) & (_iota((ncp, SEL_LANES), 1) < ns))
    cover = _mx(cover.astype(F32))
    halves = [(lane >= g * NSA_DH) & (lane < (g + 1) * NSA_DH) for g in range(NSA_GROUPS)]
    slope_col = jnp.concatenate([jnp.full((tq, 1), s, F32) for g in range(NSA_GROUPS) for s in slopes[g]], axis=0)
    imps = []
    for b in range(bb):
        kc = kcvc_ref[b, :, 0:LANE]
        vc = kcvc_ref[b, :, LANE:2 * LANE]
        rows = jnp.concatenate([jnp.where(halves[g], q_ref[b, :, c * LANE:(c + 1) * LANE], 0.0)
                                for g in range(NSA_GROUPS) for c in range(NSA_HPG)], axis=0)
        s = _dot3_nt(rows, kc) - slope_col * d_cf
        s = jnp.where(valid, s, NEG_INF)
        m = jnp.max(s, axis=-1, keepdims=True)
        m = jnp.where(m > NEG_INF, m, 0.0)
        e = jnp.where(valid, jnp.exp(s - m), 0.0)
        p = e / jnp.maximum(jnp.sum(e, axis=-1, keepdims=True), 1e-30)
        o = _dot(p, vc)
        for c in range(NSA_HPG):
            oc_ref[b, :, c * LANE:(c + 1) * LANE] = jnp.where(
                lane < NSA_DH, o[c * tq:(c + 1) * tq], o[(NSA_HPG + c) * tq:(NSA_HPG + c + 1) * tq])
        for g in range(NSA_GROUPS):
            psum = p[g * NSA_HPG * tq:(g * NSA_HPG + 1) * tq]
            for c in range(1, NSA_HPG):
                psum = psum + p[(g * NSA_HPG + c) * tq:(g * NSA_HPG + c + 1) * tq]
            ph, plo = _split2(psum)
            imps.append(jnp.dot(ph, cover, preferred_element_type=F32)
                        + jnp.dot(plo, cover, preferred_element_type=F32))
    imp = jnp.concatenate(imps, axis=0)
    n_sets = bb * NSA_GROUPS
    lane_all = _iota((n_sets * tq, SEL_LANES), 1)
    blk_q = (tbase + (_iota((n_sets * tq, 1), 0) & (tq - 1))) >> _LOG2_SEL
    sel = _topk_mask(imp, (lane_all < blk_q) & (lane_all < ns), n_top)
    if own_in_range:
        sel = jnp.where(lane_all == blk_q, 1.0, sel)
    for b in range(bb):
        for g in range(NSA_GROUPS):
            k = b * NSA_GROUPS + g
            sel_ref[b, :, g * SEL_LANES:(g + 1) * SEL_LANES] = sel[k * tq:(k + 1) * tq]


def nsa_cmp_select(q, kcvc, *, bb, tq, t0, nc, ns, own_in_range, slopes):
    nb, sq, qw = q.shape
    ncp = kcvc.shape[1]
    kern = functools.partial(_nsa_cmp_kernel, bb=bb, tq=tq, t0=t0, nc=nc, ns=ns, n_top=min(NSA_TOPN, ns),
                             own_in_range=own_in_range, slopes=slopes)
    return pl.pallas_call(
        kern,
        grid=(nb // bb, sq // tq),
        in_specs=[pl.BlockSpec((bb, tq, qw), lambda b, i: (b, i, 0)),
                  pl.BlockSpec((bb, ncp, kcvc.shape[2]), lambda b, i: (b, 0, 0))],
        out_specs=[pl.BlockSpec((bb, tq, qw), lambda b, i: (b, i, 0)),
                   pl.BlockSpec((bb, tq, NSA_GROUPS * SEL_LANES), lambda b, i: (b, i, 0))],
        out_shape=[jax.ShapeDtypeStruct((nb, sq, qw), F32),
                   jax.ShapeDtypeStruct((nb, sq, NSA_GROUPS * SEL_LANES), F32)],
        compiler_params=_params(("parallel", "parallel")),
        name="nsa_cmp_select",
    )(q, kcvc)


def _block_mean_kernel(k_ref, o_ref, *, nblk):
    o_ref[...] = jnp.zeros(o_ref.shape, F32)
    for j in range(nblk):
        blk = k_ref[j * MOBA_BLOCK:(j + 1) * MOBA_BLOCK, :]
        o_ref[j:j + 1, :] = jnp.sum(blk, axis=0, keepdims=True) * (1.0 / MOBA_BLOCK)


def moba_block_mean(k):
    nb, s, w = k.shape
    nblk = s // MOBA_BLOCK
    return pl.pallas_call(
        functools.partial(_block_mean_kernel, nblk=nblk),
        grid=(nb,),
        in_specs=[pl.BlockSpec((None, s, w), lambda b: (b, 0, 0))],
        out_specs=pl.BlockSpec((None, SEL_LANES, w), lambda b: (b, 0, 0)),
        out_shape=jax.ShapeDtypeStruct((nb, SEL_LANES, w), F32),
        compiler_params=_params(("parallel",)),
        name="moba_block_mean",
    )(k)


def _page_sum_kernel(x_ref, o_ref, *, pp):
    ones = jnp.ones((SUBLANE, x_ref.shape[2]), MXU_DTYPE)
    f = lambda a: lax.dot_general(ones, a, _NT, preferred_element_type=F32)
    x = x_ref[...].reshape(pp * x_ref.shape[1], x_ref.shape[2])
    h1, h2 = _split2(x)
    o_ref[...] = (f(h1) + f(h2))[0:1]


def page_sum(pages, layer):
    _, npg, w, ps = pages.shape
    pp = _largest_divisor(npg, CMP_PAGES)
    return pl.pallas_call(
        functools.partial(_page_sum_kernel, pp=pp),
        grid=(npg // pp,),
        in_specs=[pl.BlockSpec((None, pp, w, ps), lambda i: (layer, i, 0, 0))],
        out_specs=pl.BlockSpec((None, 1, pp * w), lambda i: (i, 0, 0)),
        out_shape=jax.ShapeDtypeStruct((npg // pp, 1, pp * w), F32),
        compiler_params=_params(("parallel",)),
        name="page_sum",
    )(pages).reshape(npg, w)


def _kmean_gather_kernel(pt_ref, tbl_ref, o_ref, *, nblk, ppb):
    b = pl.program_id(0)
    o_ref[...] = jnp.zeros(o_ref.shape, F32)
    for j in range(nblk):
        acc = tbl_ref[pl.ds(pt_ref[b, j * ppb], 1), :]
        for k in range(1, ppb):
            acc = acc + tbl_ref[pl.ds(pt_ref[b, j * ppb + k], 1), :]
        o_ref[j:j + 1, :] = acc * (1.0 / MOBA_BLOCK)


def moba_kmean_paged(page_sums, page_table):
    nb, npg = page_table.shape
    ppb = MOBA_BLOCK // PAGE_SIZE
    nblk = npg // ppb
    w = page_sums.shape[1]
    grid_spec = pltpu.PrefetchScalarGridSpec(
        num_scalar_prefetch=1,
        grid=(nb,),
        in_specs=[pl.BlockSpec(page_sums.shape, lambda b, pt: (0, 0))],
        out_specs=pl.BlockSpec((None, SEL_LANES, w), lambda b, pt: (b, 0, 0)),
    )
    return pl.pallas_call(
        functools.partial(_kmean_gather_kernel, nblk=nblk, ppb=ppb),
        grid_spec=grid_spec,
        out_shape=jax.ShapeDtypeStruct((nb, SEL_LANES, w), F32),
        compiler_params=_params(("arbitrary",)),
        name="moba_kmean_paged",
    )(page_table, page_sums)


def _moba_select_kernel(q_ref, km_ref, sel_ref, *, tq, t0, n_full, n_sel, own_in_range):
    tbase = t0 + pl.program_id(1) * tq
    q = q_ref[...]
    km = km_ref[...]
    lane_q = _iota(q.shape, 1)
    lane = _iota((tq, SEL_LANES), 1)
    blk_q = (tbase + _iota((tq, 1), 0)) >> _LOG2_MOBA
    cand = (lane < blk_q) & (lane < n_full)
    for hd in range(MOBA_HEADS):
        qh = jnp.where((lane_q >> _LOG2_DH) == hd, q, 0.0)
        gs = _dot3_nt(qh, km)
        sel = _topk_mask(gs, cand, n_sel)
        if own_in_range:
            sel = jnp.where(lane == blk_q, 1.0, sel)
        sel_ref[:, hd * SEL_LANES:(hd + 1) * SEL_LANES] = sel


def moba_select(q, kmean, *, tq, t0, n_full, own_in_range):
    nb, sq, qw = q.shape
    kern = functools.partial(_moba_select_kernel, tq=tq, t0=t0, n_full=n_full,
                             n_sel=min(MOBA_TOPK, n_full), own_in_range=own_in_range)
    return pl.pallas_call(
        kern,
        grid=(nb, sq // tq),
        in_specs=[pl.BlockSpec((None, tq, qw), lambda b, i: (b, i, 0)),
                  pl.BlockSpec((None,) + kmean.shape[1:], lambda b, i: (b, 0, 0))],
        out_specs=pl.BlockSpec((None, tq, MOBA_HEADS * SEL_LANES), lambda b, i: (b, i, 0)),
        out_shape=jax.ShapeDtypeStruct((nb, sq, MOBA_HEADS * SEL_LANES), F32),
        compiler_params=_params(("parallel", "parallel")),
        name="moba_select",
    )(q, kmean)


def _nsa_jobs(slopes):
    heads = [(g, c) for g in range(NSA_GROUPS) for c in range(NSA_HPG)]
    return [dict(qcols=[(c * LANE, LANE) for g, c in heads],
                 qmasks=[(g * NSA_DH, (g + 1) * NSA_DH) for g, c in heads], q2cols=None,
                 kcol=(0, LANE), vcol=(0, LANE), slopes=[slopes[g][c] for g, c in heads],
                 sels=[g * SEL_LANES for g, c in heads], sel_shift=_LOG2_SEL,
                 outs=[(0, c * LANE, g * NSA_DH, (g + 1) * NSA_DH) for g, c in heads])]


def _moba_jobs(slopes):
    hs = range(MOBA_HEADS)
    width = MOBA_HEADS * MOBA_DH
    return [dict(qcols=[(0, width) for _ in hs], qmasks=[(h * MOBA_DH, (h + 1) * MOBA_DH) for h in hs],
                 q2cols=None, kcol=(0, width), vcol=(0, width), slopes=[slopes[h] for h in hs],
                 sels=[h * SEL_LANES for h in hs], sel_shift=_LOG2_MOBA,
                 outs=[((h // 2) * LANE, (h // 2) * LANE, (h % 2) * MOBA_DH, (h % 2 + 1) * MOBA_DH) for h in hs])]


def _stack_q(q_ref, job, tq):
    parts = []
    for (off, w), qmask in zip(job["qcols"], job["qmasks"]):
        x = q_ref[:, off:off + w]
        if qmask is not None:
            lo, hi = qmask
            ln = _iota(x.shape, 1)
            x = jnp.where((ln >= lo) & (ln < hi), x, 0.0)
        parts.append(x)
    qs = _mx(jnp.concatenate(parts, axis=0))
    q2 = None
    if job["q2cols"] is not None:
        q2 = _mx(jnp.concatenate([q_ref[:, off:off + w] for off, w in job["q2cols"]], axis=0))
    slope_col = None
    if job["slopes"] is not None:
        slope_col = jnp.concatenate([jnp.full((tq, 1), s, F32) for s in job["slopes"]], axis=0)
    return qs, q2, slope_col


def _block_mask(sel_ref, job, k0, span):
    shift = job["sel_shift"]
    expand = _mx((_iota((SEL_LANES, span), 0) == ((k0 + _iota((SEL_LANES, span), 1)) >> shift)).astype(F32))
    ems = {}
    for soff in job["sels"]:
        if soff not in ems:
            ems[soff] = jnp.dot(_mx(sel_ref[:, soff:soff + SEL_LANES]), expand, preferred_element_type=F32)
    return jnp.concatenate([ems[soff] for soff in job["sels"]], axis=0)


def _softmax_step(s, allowed, v, m, l, acc, v_t=False):
    if allowed is not None:
        s = jnp.where(allowed, s, NEG_INF)
    m_new = jnp.maximum(m, jnp.max(s, axis=-1, keepdims=True))
    m_safe = jnp.where(m_new > NEG_INF, m_new, 0.0)
    alpha = jnp.exp(m - m_safe)
    p = jnp.exp(s - m_safe)
    l = alpha * l + jnp.sum(p, axis=-1, keepdims=True)
    acc = alpha * acc + (_dot_nt(p, v) if v_t else _dot(p, v))
    return m_new, l, acc


def _write_outputs(o_ref, jobs, results, tq):
    lane = _iota((tq, LANE), 1)
    cols = {}
    for job, res in zip(jobs, results):
        for k, (voff, off, lo, hi) in enumerate(job["outs"]):
            piece = res[k * tq:(k + 1) * tq, voff:voff + LANE]
            if (lo, hi) == (0, LANE):
                cols[off] = piece
            else:
                prev = cols.get(off, jnp.zeros((tq, LANE), F32))
                cols[off] = jnp.where((lane >= lo) & (lane < hi), piece, prev)
    for off, val in cols.items():
        o_ref[:, off:off + LANE] = val


def _flash_kernel(*refs, jobs, tq, tk, q_pos0, window, has_sel, has_v):
    refs = list(refs)
    q_ref = refs.pop(0)
    k_ref = refs.pop(0)
    v_ref = refs.pop(0) if has_v else k_ref
    sel_ref = refs.pop(0) if has_sel else None
    o_ref = refs.pop(0)
    t_lo = q_pos0 + pl.program_id(1) * tq
    n_kt = k_ref.shape[0] // tk
    kt_hi = jnp.minimum((t_lo + tq - 1) // tk + 1, n_kt)
    kt_lo = jnp.maximum(t_lo - window + 1, 0) // tk if window else 0
    results = []
    for job in jobs:
        qs, q2, slope_col = _stack_q(q_ref, job, tq)
        nq = len(job["qcols"])
        r = nq * tq
        t_rows = t_lo + (_iota((r, 1), 0) & (tq - 1))
        koff, kw = job["kcol"]
        voff, vw = job["vcol"]
        masked = has_sel and job["sels"] is not None
        t_win = t_rows - window

        def body(kt, carry, causal, qs=qs, slope_col=slope_col, t_rows=t_rows, t_win=t_win, koff=koff, kw=kw,
                 voff=voff, vw=vw, masked=masked, job=job):
            m, l, acc = carry
            k0 = pl.multiple_of(kt * tk, tk)
            ktile = k_ref[pl.ds(k0, tk), koff:koff + kw]
            vtile = v_ref[pl.ds(k0, tk), voff:voff + vw]
            s = _dot_nt(qs, ktile)
            kpos = k0 + _iota((1, tk), 1)
            allowed = None
            if causal:
                allowed = t_rows >= kpos
                if window:
                    allowed = allowed & (t_win < kpos)
            if slope_col is not None:
                s = s - slope_col * (t_lo - kpos).astype(F32)
            if masked:
                picked = _block_mask(sel_ref, job, k0, tk) > 0.5
                allowed = picked if allowed is None else allowed & picked
            return _softmax_step(s, allowed, vtile, m, l, acc)

        init = (jnp.full((r, 1), NEG_INF, F32), jnp.zeros((r, 1), F32), jnp.zeros((r, vw), F32))
        if window:
            m, l, acc = lax.fori_loop(kt_lo, kt_hi, functools.partial(body, causal=True), init)
        else:
            kt_mid = jnp.clip((t_lo + 1) // tk, kt_lo, kt_hi)
            carry = lax.fori_loop(kt_lo, kt_mid, functools.partial(body, causal=False), init)
            m, l, acc = lax.fori_loop(kt_mid, kt_hi, functools.partial(body, causal=True), carry)
        results.append(acc / jnp.maximum(l, 1e-30))
    _write_outputs(o_ref, jobs, results, tq)


def flash_attention(q, k, v, sel, *, jobs, out_width, tq, tk, q_pos0=0, window=0):
    nb, sq, qw = q.shape
    sk = k.shape[1]
    ops = [q, k]
    specs = [pl.BlockSpec((None, tq, qw), lambda b, i: (b, i, 0)),
             pl.BlockSpec((None, sk, k.shape[2]), lambda b, i: (b, 0, 0))]
    if v is not None:
        ops.append(v)
        specs.append(pl.BlockSpec((None, sk, v.shape[2]), lambda b, i: (b, 0, 0)))
    if sel is not None:
        ops.append(sel)
        specs.append(pl.BlockSpec((None, tq, sel.shape[2]), lambda b, i: (b, i, 0)))
    kern = functools.partial(_flash_kernel, jobs=jobs, tq=tq, tk=tk, q_pos0=q_pos0, window=window,
                             has_sel=sel is not None, has_v=v is not None)
    return pl.pallas_call(
        kern,
        grid=(nb, sq // tq),
        in_specs=specs,
        out_specs=pl.BlockSpec((None, tq, out_width), lambda b, i: (b, i, 0)),
        out_shape=jax.ShapeDtypeStruct((nb, sq, out_width), F32),
        compiler_params=_params(("parallel", "parallel")),
        name="flash_attention",
    )(*ops)


def _paged_kernel(pt_ref, *refs, jobs, npages, page_len, past_len, key_pos0, window, n_new, has_sel, has_k2,
                  has_v, k_t, k2_t, v_t):
    refs = list(refs)
    q_ref = refs.pop(0)
    sel_ref = refs.pop(0) if has_sel else None
    knew_ref = refs.pop(0)
    k2new_ref = refs.pop(0) if has_k2 else None
    vnew_ref = refs.pop(0) if has_v else knew_ref
    k_refs = [refs.pop(0) for _ in range(npages)]
    k2_refs = [refs.pop(0) for _ in range(npages)] if has_k2 else None
    v_refs = [refs.pop(0) for _ in range(npages)] if has_v else k_refs
    o_ref = refs.pop(0)
    m_scr, l_scr, acc_scr = refs
    j = pl.program_id(1)
    tq = q_ref.shape[0]
    span = npages * page_len

    @pl.when(j == 0)
    def _():
        m_scr[...] = jnp.full(m_scr.shape, NEG_INF, F32)
        l_scr[...] = jnp.zeros(l_scr.shape, F32)
        acc_scr[...] = jnp.zeros(acc_scr.shape, F32)

    cat = lambda rs, t: jnp.concatenate([r[...] for r in rs], axis=1 if t else 0)
    kcat = cat(k_refs, k_t)
    vcat = cat(v_refs, v_t) if has_v else kcat
    k2cat = _mx(cat(k2_refs, k2_t)) if has_k2 else None
    k0 = j * span
    kpos = key_pos0 + k0 + _iota((1, span), 1)
    row0 = 0
    stacked = []
    for job in jobs:
        qs, q2, slope_col = _stack_q(q_ref, job, tq)
        nq = len(job["qcols"])
        r = nq * tq
        t_rows = past_len + (_iota((r, 1), 0) & (tq - 1))
        koff, kw = job["kcol"]
        voff, vw = job["vcol"]
        s = _dot(qs, kcat[koff:koff + kw, :]) if k_t else _dot_nt(qs, kcat[:, koff:koff + kw])
        if q2 is not None:
            s = s + (jnp.dot(q2, k2cat, preferred_element_type=F32) if k2_t
                     else lax.dot_general(q2, k2cat, _NT, preferred_element_type=F32))
        allowed = t_rows >= kpos
        if window:
            allowed = allowed & ((t_rows - window) < kpos)
        if slope_col is not None:
            s = s - slope_col * (past_len - kpos).astype(F32)
        if has_sel and job["sels"] is not None:
            allowed = allowed & (_block_mask(sel_ref, job, k0, span) > 0.5)
        rows = slice(row0, row0 + r)
        vsl = vcat[voff:voff + vw, :] if v_t else vcat[:, voff:voff + vw]
        m, l, acc = _softmax_step(s, allowed, vsl, m_scr[rows], l_scr[rows], acc_scr[rows], v_t)
        m_scr[rows] = m
        l_scr[rows] = l
        acc_scr[rows] = acc
        stacked.append((qs, q2, slope_col, t_rows, rows, r))
        row0 += r

    @pl.when(j == pl.num_programs(1) - 1)
    def _():
        results = []
        n_rows = knew_ref.shape[0]
        cpos = _iota((1, n_rows), 1)
        for job, (qs, q2, slope_col, t_rows, rows, r) in zip(jobs, stacked):
            koff, kw = job["kcol"]
            voff, vw = job["vcol"]
            s = _dot_nt(qs, knew_ref[:, koff:koff + kw])
            if q2 is not None:
                s = s + _dot_nt(q2, k2new_ref[...])
            allowed = (t_rows >= past_len + cpos) & (cpos < n_new)
            if slope_col is not None:
                s = s + slope_col * cpos.astype(F32)
            m, l, acc = _softmax_step(s, allowed, vnew_ref[:, voff:voff + vw],
                                      m_scr[rows], l_scr[rows], acc_scr[rows])
            results.append(acc / jnp.maximum(l, 1e-30))
        _write_outputs(o_ref, jobs, results, tq)


def paged_attention(q, sel, k_new, k2_new, v_new, k_pool, k2_pool, v_pool, page_table, layer, *,
                    jobs, out_width, past_len, n_new, k_t, k2_t=False, v_t=False, key_pos0=0, window=0,
                    pages_per_step=DEC_PAGES):
    nb, tq, qw = q.shape
    npg = page_table.shape[1]
    p = _largest_divisor(npg, pages_per_step)
    page_len = k_pool.shape[3] if k_t else k_pool.shape[2]
    if v_pool is None:
        v_t = k_t
    seq = lambda a: pl.BlockSpec((None,) + a.shape[1:], lambda b, j, pt: (b, 0, 0))
    page = lambda pool, k: pl.BlockSpec((None, None) + pool.shape[2:],
                                        lambda b, j, pt: (layer, pt[b, j * p + k], 0, 0))
    ops, specs = [q], [seq(q)]
    if sel is not None:
        ops.append(sel)
        specs.append(seq(sel))
    ops.append(k_new)
    specs.append(seq(k_new))
    if k2_new is not None:
        ops.append(k2_new)
        specs.append(seq(k2_new))
    if v_new is not None:
        ops.append(v_new)
        specs.append(seq(v_new))
    for pool in (k_pool, k2_pool, v_pool):
        if pool is not None:
            ops += [pool] * p
            specs += [page(pool, k) for k in range(p)]
    rows = sum(len(job["qcols"]) for job in jobs) * tq
    acc_w = max(job["vcol"][1] for job in jobs)
    grid_spec = pltpu.PrefetchScalarGridSpec(
        num_scalar_prefetch=1,
        grid=(nb, npg // p),
        in_specs=specs,
        out_specs=pl.BlockSpec((None, tq, out_width), lambda b, j, pt: (b, 0, 0)),
        scratch_shapes=[pltpu.VMEM((rows, 1), F32), pltpu.VMEM((rows, 1), F32), pltpu.VMEM((rows, acc_w), F32)],
    )
    kern = functools.partial(_paged_kernel, jobs=jobs, npages=p, page_len=page_len, past_len=past_len,
                             key_pos0=key_pos0, window=window, n_new=n_new, has_sel=sel is not None,
                             has_k2=k2_pool is not None, has_v=v_pool is not None, k_t=k_t, k2_t=k2_t, v_t=v_t)
    return pl.pallas_call(
        kern,
        grid_spec=grid_spec,
        out_shape=jax.ShapeDtypeStruct((nb, tq, out_width), F32),
        compiler_params=_params(("parallel", "arbitrary")),
        name="paged_attention",
    )(page_table, *ops)


def _merge_kernel(x_ref, sh_ref, sc_ref, gt_ref, g_ref, oc_ref, os_ref, ow_ref, gate_ref, ctx_ref, om_ref,
                  wmg_ref, wa_ref, wuv_ref, wb_ref, wc_ref, wo_ref, o_ref):
    x = x_ref[...]
    h = _modulate(x, g_ref[...], sh_ref[...], sc_ref[...])
    mg = _sigmoid(_dot(h, wmg_ref[...]))
    n_chunk = NSA_HEADS
    width = n_chunk * NSA_DH
    rowi = _iota((LANE, N_BRANCH * width), 0)
    coli = _iota((LANE, N_BRANCH * width), 1)
    expand = _mx((rowi == (coli >> _LOG2_DH)).astype(F32))
    ge = _dot_exact01(gate_ref[...], expand)
    oa = ge[:, 0:width] * oc_ref[...] + ge[:, width:2 * width] * os_ref[...] + ge[:, 2 * width:] * ow_ref[...]
    ob = _dot(ctx_ref[...], wuv_ref[...])
    d = x.shape[1]
    mixed = (mg[:, 0:d] * _dot(oa, wa_ref[...]) + mg[:, d:2 * d] * _dot(ob, wb_ref[...])
             + mg[:, 2 * d:] * _dot(om_ref[...], wc_ref[...]))
    o_ref[...] = x + gt_ref[...] * _dot(mixed, wo_ref[...])


def merge(x, mod_ops, norm_g, oc, os_, ow, gate, ctx, om, lw, tm):
    t, d = x.shape
    full = lambda a: pl.BlockSpec(a.shape, lambda i: (0,) * a.ndim)
    row = lambda a: pl.BlockSpec((tm, a.shape[1]), lambda i: (i, 0))
    (sh, sh_s), (sc, sc_s), (gt, gt_s) = mod_ops[0], mod_ops[1], mod_ops[2]
    ws = [lw["w_mg"], lw["w_br_nsa"], lw["w_uv"], lw["w_br_mla"], lw["w_br_moba"], lw["w_out"]]
    acts = [oc, os_, ow, gate, ctx, om]
    return pl.pallas_call(
        _merge_kernel,
        grid=(t // tm,),
        in_specs=[row(x), sh_s, sc_s, gt_s, full(norm_g)] + [row(a) for a in acts] + [full(w) for w in ws],
        out_specs=row(x),
        out_shape=jax.ShapeDtypeStruct((t, d), F32),
        compiler_params=_params(("parallel",)),
        name="merge",
    )(x, sh, sc, gt, norm_g, *acts, *ws)


def _ffn_kernel(x_ref, sh_ref, sc_ref, gt_ref, g_ref, wi_ref, wo_ref, fn_ref, o_ref, *, final):
    x = x_ref[...]
    h = _modulate(x, g_ref[...], sh_ref[...], sc_ref[...])
    au = _dot(h, wi_ref[...])
    hid = au.shape[1] // 2
    y = x + gt_ref[...] * _dot(_silu(au[:, :hid]) * au[:, hid:], wo_ref[...])
    o_ref[...] = _rms(y, fn_ref[...]) if final else y


def ffn(x, mod_ops, norm_g, lw, final_norm, final, tm):
    t, d = x.shape
    full = lambda a: pl.BlockSpec(a.shape, lambda i: (0,) * a.ndim, pipeline_mode=pl.Buffered(1))
    row = pl.BlockSpec((tm, d), lambda i: (i, 0))
    (sh, sh_s), (sc, sc_s), (gt, gt_s) = mod_ops[3], mod_ops[4], mod_ops[5]
    return pl.pallas_call(
        functools.partial(_ffn_kernel, final=final),
        grid=(t // tm,),
        in_specs=[row, sh_s, sc_s, gt_s, full(norm_g), full(lw["ffn_in"]), full(lw["ffn_out"]), full(final_norm)],
        out_specs=row,
        out_shape=jax.ShapeDtypeStruct((t, d), F32),
        compiler_params=_params(("parallel",)),
        name="ffn",
    )(x, sh, sc, gt, norm_g, lw["ffn_in"], lw["ffn_out"], final_norm)


def _prep_layer(l, w_in, mla_q_norm, mla_w_uq, mla_kv_norm, mla_w_uk, mla_w_uv, nsa_cmp_w1, nsa_cmp_pe,
                nsa_cmp_w2, w_br_nsa, w_br_mla, w_br_moba, w_out, ffn_w_in, ffn_w_out):
    d = w_in.shape[1]
    g, hpg, dh = NSA_GROUPS, NSA_HPG, NSA_DH
    cuts = np.cumsum(IN_WIDTHS)[:-1].tolist()
    nq, nkv, ng, mql, mkvl, mkpe, mqkv, mg = jnp.split(w_in[l], cuts, axis=1)
    nq_p = nq.reshape(d, g, hpg, dh).transpose(0, 2, 1, 3).reshape(d, g * hpg * dh)
    ng_p = ng.reshape(d, g, hpg, N_BRANCH).transpose(0, 3, 2, 1).reshape(d, N_BRANCH * g * hpg)
    ng_p = jnp.pad(ng_p, ((0, 0), (0, LANE - ng_p.shape[1])))
    half = MLA_ROPE // 2
    swap = lambda a: jnp.concatenate([a[..., half:], a[..., :half]], axis=-1)
    rep = LANE // MLA_ROPE
    f1 = jnp.tile(mkpe, (1, rep))
    f2 = jnp.tile(swap(mkpe), (1, rep))
    w_a = jnp.concatenate([nq_p, nkv, ng_p, mql, mkvl, f1, f2, mqkv], axis=1)
    uq = mla_w_uq[l]
    uq_nope = uq[:, :, :MLA_NOPE].reshape(MLA_Q_LORA, MLA_HEADS * MLA_NOPE)
    uq_pe = uq[:, :, MLA_NOPE:]
    w_uq = jnp.concatenate([uq_nope, uq_pe.reshape(MLA_Q_LORA, -1), swap(uq_pe).reshape(MLA_Q_LORA, -1)], axis=1)
    uk = mla_w_uk[l]
    w_uk = block_diag(*[uk[:, hd, :].T for hd in range(MLA_HEADS)])
    uv = mla_w_uv[l]
    w_uv = block_diag(*[uv[:, hd, :] for hd in range(MLA_HEADS)])
    w1 = nsa_cmp_w1[l]
    hid = w1.shape[-1]
    half_c = NSA_CMP_STRIDE // 2
    base = w1.reshape(2, NSA_CMP_R, half_c, 2, dh, hid).transpose(0, 2, 3, 4, 1, 5)
    wc = jnp.zeros((2, half_c, 2, g, dh, g, NSA_CMP_R, hid), F32)
    for gi in range(g):
        wc = wc.at[:, :, :, gi, :, gi, :, :].set(base)
    w_chunks = wc.reshape(2, half_c, 2 * g * dh, g * NSA_CMP_R * hid)
    pe = nsa_cmp_pe[l].reshape(2, NSA_CMP_R, NSA_CMP_STRIDE, dh)
    pe_h = jnp.einsum('krcd,krcdh->kh', pe, w1, precision=lax.Precision.HIGHEST)
    pe_const = jnp.concatenate([pe_h[kv] for kv in range(2) for _ in range(g)])[None, :]
    w2 = nsa_cmp_w2[l]
    w2_bd = block_diag(*[w2[kv] for kv in range(2) for _ in range(g)])
    br_nsa = w_br_nsa[l].reshape(g, hpg, dh, d).transpose(1, 0, 2, 3).reshape(g * hpg * dh, d)
    return dict(
        w_a=_mx(w_a), qn=mla_q_norm[l][None, :], w_uq=_mx(w_uq), w_uk=_mx(w_uk), kvn=mla_kv_norm[l][None, :],
        w_chunks=_mx(w_chunks), pe_const=pe_const, w2_bd=_mx(w2_bd), w_mg=_mx(mg), w_br_nsa=_mx(br_nsa),
        w_uv=_mx(w_uv), w_br_mla=_mx(w_br_mla[l]), w_br_moba=_mx(w_br_moba[l]), w_out=_mx(w_out[l]),
        ffn_in=_mx(ffn_w_in[l]), ffn_out=_mx(ffn_w_out[l]))


def _rope_tables(pos):
    half = MLA_ROPE // 2
    inv = ROPE_THETA ** (-jnp.arange(half, dtype=F32) / half)
    ang = pos.astype(F32)[:, None] * inv
    cos, sin = jnp.cos(ang), jnp.sin(ang)
    rep = LANE // MLA_ROPE
    return (jnp.tile(jnp.concatenate([cos, cos], axis=1), (1, rep)),
            jnp.tile(jnp.concatenate([-sin, sin], axis=1), (1, rep)))


def _pad_rows(a, rows):
    return jnp.pad(a, ((0, 0), (0, rows - a.shape[1]), (0, 0)))


def kernel(x_prompt, x_sample, c_prompt, c_sample, cache_nsa_cmp_kv, cache_nsa_slc_k, cache_nsa_slc_v, state_nsa_win_kv, cache_mla_latent, cache_mla_kpe, cache_moba_k, cache_moba_v, page_table, ada_w, ada_b, norm_mix, norm_ffn, w_in, nsa_cmp_w1, nsa_cmp_pe, nsa_cmp_w2, mla_q_norm, mla_w_uq, mla_kv_norm, mla_w_uk, mla_w_uv, w_br_nsa, w_br_mla, w_br_moba, w_out, ffn_w_in, ffn_w_out, final_norm):
    depth = w_in.shape[0]
    nb, seq, d = x_prompt.shape
    db, ds, _ = x_sample.shape
    npg = page_table.shape[1]
    past_len = npg * PAGE_SIZE
    win_len = state_nsa_win_kv.shape[2]
    tp, ts = nb * seq, db * ds
    tm = TOKEN_TILE
    tm_s = min(TOKEN_TILE, ts)
    assert seq % MOBA_BLOCK == 0 and seq % ATT_TK_LONG == 0 and seq % (2 * ATT_TQ) == 0 and seq % tm == 0 and ts % tm_s == 0 and tm_s % SUBLANE == 0
    assert past_len % MOBA_BLOCK == 0 and ds <= DEC_Q and win_len == NSA_WINDOW <= past_len
    assert past_len // NSA_SEL_BLOCK <= SEL_LANES and seq // NSA_SEL_BLOCK <= SEL_LANES
    nsa_sl, moba_sl = alibi_slopes()
    nsa_jobs, moba_jobs = _nsa_jobs(nsa_sl), _moba_jobs(moba_sl)
    win_jobs = [dict(j, sels=None, vcol=(LANE, LANE)) for j in nsa_jobs]
    mla_jobs = [dict(qcols=[(hd * 256, 256) for hd in range(MLA_HEADS)], qmasks=[None] * MLA_HEADS, q2cols=None,
                     kcol=(0, 256), vcol=(0, LANE), slopes=None, sels=None, sel_shift=None,
                     outs=[(0, hd * LANE, 0, LANE) for hd in range(MLA_HEADS)])]
    mla_dec_jobs = [dict(mla_jobs[0], qcols=[(hd * 256, LANE) for hd in range(MLA_HEADS)], kcol=(0, LANE),
                         q2cols=[(hd * 256 + LANE + hd * MLA_ROPE, MLA_ROPE) for hd in range(MLA_HEADS)])]

    n_c = nb + db
    c_all = jnp.pad(jnp.concatenate([c_prompt, c_sample], axis=0), ((0, -n_c % SUBLANE), (0, 0)))
    mod_all = adaln_all(c_all, ada_w, ada_b)

    def mod_ops_prompt(l):
        arr = mod_all[l, :nb].reshape(nb * N_MOD, 1, d)
        tiles = seq // tm
        return [(arr, pl.BlockSpec((None, 1, d), lambda i, k=k: ((i // tiles) * N_MOD + k, 0, 0)))
                for k in range(N_MOD)]

    def mod_ops_sample(l):
        arr = jnp.repeat(mod_all[l, nb:n_c].reshape(db, N_MOD, d), ds, axis=0).transpose(1, 0, 2)
        return [(arr, pl.BlockSpec((None, tm_s, d), lambda i, k=k: (k, i, 0))) for k in range(N_MOD)]

    c4p, s4p = _rope_tables(jnp.arange(seq))
    c4s, s4s = _rope_tables(jnp.tile(past_len + jnp.arange(ds), db))
    tiles_p = seq // tm
    tbl_spec_p = pl.BlockSpec((tm, LANE), lambda i: (i % tiles_p, 0))
    tbl_spec_s = pl.BlockSpec((tm_s, LANE), lambda i: (i, 0))

    def pages_t(a):
        nd = a.ndim
        return jnp.transpose(a, (0, 1) + tuple(range(3, nd)) + (2,)).reshape(a.shape[0], a.shape[1], -1, a.shape[2])

    cmp_pool = pages_t(cache_nsa_cmp_kv)
    slc_k_pool, slc_v_pool = pages_t(cache_nsa_slc_k), pages_t(cache_nsa_slc_v)
    moba_k_pool, moba_v_pool = pages_t(cache_moba_k), pages_t(cache_moba_v)
    kpe_pool = pages_t(cache_mla_kpe)
    win_pool = pages_t(state_nsa_win_kv)
    ident_win = jnp.arange(db, dtype=jnp.int32).reshape(db, 1)

    xp = x_prompt.reshape(tp, d)
    xs = x_sample.reshape(ts, d)
    st_p, st_s = [], []
    for l in range(depth):
        lw = _prep_layer(l, w_in, mla_q_norm, mla_w_uq, mla_kv_norm, mla_w_uk, mla_w_uv, nsa_cmp_w1, nsa_cmp_pe,
                         nsa_cmp_w2, w_br_nsa, w_br_mla, w_br_moba, w_out, ffn_w_in, ffn_w_out)
        g_mix, g_ffn = norm_mix[l][None, :], norm_ffn[l][None, :]
        last = l == depth - 1
        fin = final_norm[None, :]

        mods = mod_ops_prompt(l)
        (q, cmp, sk, sv, win, gate, qmla, latkpe, mq, mk, mv) = inproj(xp, mods, g_mix, lw, c4p, s4p, tbl_spec_p, tm)
        b3 = lambda a: a.reshape(nb, seq, a.shape[-1])
        a_chunks = cmp_chunk_proj(cmp.reshape(1, tp // PAGE_SIZE, PAGE_SIZE, cmp.shape[-1]), 0, lw["w_chunks"])
        kcvc = cmp_assemble(a_chunks.reshape(nb, seq // NSA_CMP_STRIDE, -1), lw["pe_const"], lw["w2_bd"])
        oc, sel = nsa_cmp_select(b3(q), kcvc, bb=1, tq=ATT_TQ, t0=0, nc=seq // NSA_CMP_STRIDE - 1,
                                 ns=seq // NSA_SEL_BLOCK, own_in_range=True, slopes=nsa_sl)
        os_ = flash_attention(b3(q), b3(sk), b3(sv), sel, jobs=nsa_jobs, out_width=512, tq=ATT_TQ, tk=ATT_TK_LONG)
        ow = flash_attention(b3(q), b3(win), None, None, jobs=win_jobs, out_width=512, tq=ATT_TQ, tk=ATT_TK,
                             window=NSA_WINDOW)
        ctx = flash_attention(b3(qmla), b3(latkpe), None, None, jobs=mla_jobs, out_width=512, tq=2 * ATT_TQ,
                              tk=ATT_TK_LONG)
        msel = moba_select(b3(mq), moba_block_mean(b3(mk)), tq=ATT_TQ, t0=0, n_full=seq // MOBA_BLOCK,
                           own_in_range=True)
        om = flash_attention(b3(mq), b3(mk), b3(mv), msel, jobs=moba_jobs, out_width=256, tq=2 * ATT_TQ,
                             tk=ATT_TK_LONG)
        f2 = lambda a: a.reshape(tp, a.shape[-1])
        x1 = merge(xp, mods, g_mix, f2(oc), f2(os_), f2(ow), gate, f2(ctx), f2(om), lw, tm)
        xp = ffn(x1, mods, g_ffn, lw, fin, last, tm)
        win_keep = min(NSA_WINDOW, seq)
        st_p.append((cmp.reshape(nb, seq, 2, NSA_GROUPS, NSA_DH), sk.reshape(nb, seq, NSA_GROUPS, NSA_DH),
                     sv.reshape(nb, seq, NSA_GROUPS, NSA_DH),
                     b3(win)[:, seq - win_keep:].reshape(nb, win_keep, 2, NSA_GROUPS, NSA_DH),
                     b3(latkpe)[:, :, :MLA_KV_LORA], b3(latkpe)[:, :, LANE:LANE + MLA_ROPE],
                     mk.reshape(nb, seq, MOBA_HEADS, MOBA_DH), mv.reshape(nb, seq, MOBA_HEADS, MOBA_DH)))

        mods = mod_ops_sample(l)
        (q, cmp, sk, sv, win, gate, qmla, latkpe, mq, mk, mv) = inproj(xs, mods, g_mix, lw, c4s, s4s, tbl_spec_s, tm_s)
        s3 = lambda a: a.reshape(db, ds, a.shape[-1])
        q8, qmla8, mq8 = (_pad_rows(s3(a), DEC_Q) for a in (q, qmla, mq))
        new16 = lambda a: _pad_rows(s3(a), DEC_NEW)
        a_pool = cmp_chunk_proj_t(cmp_pool, l, page_table.reshape(-1), lw["w_chunks"])
        kcvc = cmp_assemble(a_pool.reshape(db, past_len // NSA_CMP_STRIDE, -1), lw["pe_const"], lw["w2_bd"])
        oc8, sel8 = nsa_cmp_select(q8, kcvc, bb=_largest_divisor(db, DEC_SEQS), tq=DEC_Q, t0=past_len,
                                   nc=past_len // NSA_CMP_STRIDE - 1,
                                   ns=past_len // NSA_SEL_BLOCK, own_in_range=False, slopes=nsa_sl)
        os8 = paged_attention(q8, sel8, new16(sk), None, new16(sv), slc_k_pool, None, slc_v_pool, page_table, l,
                              jobs=nsa_jobs, out_width=512, past_len=past_len, n_new=ds, k_t=True, v_t=True,
                              pages_per_step=DEC_PAGES_SMALL)
        ow8 = paged_attention(q8, None, new16(win), None, None, win_pool, None, None, ident_win, l,
                              jobs=win_jobs, out_width=512, past_len=past_len, n_new=ds, k_t=True,
                              key_pos0=past_len - win_len, window=NSA_WINDOW)
        lat_new = new16(latkpe)
        ctx8 = paged_attention(qmla8, None, lat_new[:, :, :LANE], lat_new[:, :, LANE:LANE + MLA_ROPE], None,
                               cache_mla_latent, kpe_pool, None, page_table, l,
                               jobs=mla_dec_jobs, out_width=512, past_len=past_len, n_new=ds, k_t=False, k2_t=True,
                               pages_per_step=DEC_PAGES_SMALL)
        kmean = moba_kmean_paged(page_sum(moba_k_pool, l), page_table)
        msel8 = moba_select(mq8, kmean, tq=DEC_Q, t0=past_len, n_full=past_len // MOBA_BLOCK, own_in_range=False)
        om8 = paged_attention(mq8, msel8, new16(mk), None, new16(mv), moba_k_pool, None, moba_v_pool, page_table, l,
                              jobs=moba_jobs, out_width=256, past_len=past_len, n_new=ds, k_t=True, v_t=True)
        win_buf = state_nsa_win_kv[l].reshape(db, win_len, 2 * NSA_GROUPS * NSA_DH)
        win_cat = jnp.concatenate([win_buf, s3(win)], axis=1)
        d2 = lambda a: a[:, :ds].reshape(ts, a.shape[-1])
        x1 = merge(xs, mods, g_mix, d2(oc8), d2(os8), d2(ow8), gate, d2(ctx8), d2(om8), lw, tm_s)
        xs = ffn(x1, mods, g_ffn, lw, fin, last, tm_s)
        keep = min(NSA_WINDOW, past_len + ds)
        new_win = win_cat[:, -keep:].reshape(db, keep, 2, NSA_GROUPS, NSA_DH)
        st_s.append((cmp.reshape(db, ds, 2, NSA_GROUPS, NSA_DH), sk.reshape(db, ds, NSA_GROUPS, NSA_DH),
                     sv.reshape(db, ds, NSA_GROUPS, NSA_DH), new_win,
                     s3(latkpe)[:, :, :MLA_KV_LORA], s3(latkpe)[:, :, LANE:LANE + MLA_ROPE],
                     mk.reshape(db, ds, MOBA_HEADS, MOBA_DH), mv.reshape(db, ds, MOBA_HEADS, MOBA_DH)))

    outs_p = [jnp.stack(a, axis=0) for a in zip(*st_p)]
    outs_s = [jnp.stack(a, axis=0) for a in zip(*st_s)]
    return (xp.reshape(nb, seq, d), xs.reshape(db, ds, d), *outs_p, *outs_s)
```

```python
import functools

import jax
import jax.numpy as jnp
from jax import lax
import numpy as np
from jax.experimental import pallas as pl
from jax.experimental.pallas import tpu as pltpu
from jax.scipy.linalg import block_diag

D_MODEL = 1024
PAGE_SIZE = 128
NSA_HEADS = 8
NSA_GROUPS = 2
NSA_HPG = NSA_HEADS // NSA_GROUPS
NSA_DH = 64
NSA_CMP_LEN = 32
NSA_CMP_STRIDE = 16
NSA_CMP_R = NSA_CMP_LEN // NSA_CMP_STRIDE
NSA_CMP_HID = 2 * NSA_DH
NSA_SEL_BLOCK = 64
NSA_TOPN = 16
NSA_WINDOW = 512
MLA_HEADS = 4
MLA_Q_LORA = 256
MLA_KV_LORA = 128
MLA_NOPE = 64
MLA_ROPE = 32
MLA_V = 64
MLA_SCALE = (MLA_NOPE + MLA_ROPE) ** -0.5
ROPE_THETA = 10000.0
MOBA_HEADS = 4
MOBA_DH = 64
MOBA_BLOCK = 256
MOBA_TOPK = 3
FFN_HIDDEN = ((8 * D_MODEL + 3 * 256 - 1) // (3 * 256)) * 256
N_BRANCH = 3
N_MOD = 6
N_ALIBI = NSA_HEADS + MOBA_HEADS
RMS_EPS = 1e-6
IN_WIDTHS = (NSA_HEADS * NSA_DH, N_BRANCH * 2 * NSA_GROUPS * NSA_DH, N_BRANCH * NSA_HEADS,
             MLA_Q_LORA, MLA_KV_LORA, MLA_ROPE, 3 * MOBA_HEADS * MOBA_DH, N_BRANCH * D_MODEL)
NSA_SCALE = NSA_DH ** -0.5
MOBA_SCALE = MOBA_DH ** -0.5
_LOG2_DH = 6
_LOG2_SEL = 6
_LOG2_MOBA = 8
_LOG2_ROPE = 5
assert (1 << _LOG2_DH == NSA_DH == MOBA_DH and 1 << _LOG2_SEL == NSA_SEL_BLOCK
        and 1 << _LOG2_MOBA == MOBA_BLOCK and 1 << _LOG2_ROPE == MLA_ROPE)

F32 = jnp.float32
MXU_DTYPE = jnp.bfloat16
LANE = 128
SUBLANE = 8
VMEM_LIMIT = 56 * 1024 * 1024
TOKEN_TILE = 256
ATT_TQ = 128
ATT_TK = 256
ATT_TK_LONG = 512
DEC_Q = 8
DEC_NEW = 16
DEC_PAGES = 32
DEC_PAGES_SMALL = 64
DEC_SEQS = 8
CMP_PAGES = 32
CMP_PAGES_PAGED = 64
SEL_LANES = LANE
NEG_INF = float("-inf")


def _params(sem):
    return pltpu.CompilerParams(dimension_semantics=sem, vmem_limit_bytes=VMEM_LIMIT)


def _mx(x):
    return x.astype(MXU_DTYPE)


def _dot(a, b):
    return jnp.dot(_mx(a), _mx(b), preferred_element_type=F32)


_NT = (((1,), (1,)), ((), ()))


def _dot_nt(a, b):
    return lax.dot_general(_mx(a), _mx(b), _NT, preferred_element_type=F32)


def _split2(x):
    hi = _mx(x)
    lo = _mx(x - hi.astype(F32))
    return hi, lo


def _dot3_nt(a, b):
    ah, al = _split2(a)
    bh, bl = _split2(b)
    f = lambda x, y: lax.dot_general(x, y, _NT, preferred_element_type=F32)
    return f(ah, bh) + (f(ah, bl) + f(al, bh))


def _dot_exact01(x, e):
    h1 = _mx(x)
    r1 = x - h1.astype(F32)
    h2 = _mx(r1)
    h3 = _mx(r1 - h2.astype(F32))
    f = lambda a: jnp.dot(a, e, preferred_element_type=F32)
    return f(h1) + (f(h2) + f(h3))


def _sigmoid(x):
    return 1.0 / (1.0 + jnp.exp(-x))


def _silu(x):
    return x * _sigmoid(x)


def _rms(x, g):
    return x * lax.rsqrt(jnp.mean(x * x, axis=-1, keepdims=True) + RMS_EPS) * g


def _modulate(x, g, shift, scale):
    return _rms(x, g) * (1.0 + scale) + shift


def _iota(shape, dim):
    return lax.broadcasted_iota(jnp.int32, shape, dim)


def _topk_mask(vals, cand, k):
    lane = _iota(vals.shape, 1).astype(F32)
    cur = jnp.where(cand, vals, NEG_INF)
    sel = jnp.zeros(vals.shape, F32)
    for _ in range(k):
        mx = jnp.max(cur, axis=-1, keepdims=True)
        hit = (cur == mx) & (cur > NEG_INF)
        idx = jnp.min(jnp.where(hit, lane, 1e9), axis=-1, keepdims=True)
        one = lane == idx
        sel = jnp.where(one, 1.0, sel)
        cur = jnp.where(one, NEG_INF, cur)
    return sel


def alibi_slopes():
    s = 2.0 ** (-8.0 * np.arange(1, N_ALIBI + 1) / N_ALIBI)
    step = N_ALIBI // MOBA_HEADS
    moba_idx = np.arange(MOBA_HEADS) * step + step - 1
    nsa_idx = np.setdiff1d(np.arange(N_ALIBI), moba_idx)
    nsa = np.asarray(s[nsa_idx], np.float32).reshape(NSA_GROUPS, NSA_HPG)
    moba = np.asarray(s[moba_idx], np.float32)
    return [[float(v) for v in row] for row in nsa], [float(v) for v in moba]


def _adaln_kernel(c_ref, w_ref, b_ref, o_ref):
    o_ref[...] = _dot(_silu(c_ref[...]), w_ref[...]) + b_ref[...]


def adaln_all(c_all, ada_w, ada_b):
    depth, d, n = ada_w.shape
    r = c_all.shape[0]
    tn = 1024
    return pl.pallas_call(
        _adaln_kernel,
        grid=(depth, n // tn),
        in_specs=[pl.BlockSpec((r, d), lambda l, j: (0, 0)),
                  pl.BlockSpec((None, d, tn), lambda l, j: (l, 0, j)),
                  pl.BlockSpec((None, 1, tn), lambda l, j: (l, 0, j))],
        out_specs=pl.BlockSpec((None, r, tn), lambda l, j: (l, 0, j)),
        out_shape=jax.ShapeDtypeStruct((depth, r, n), F32),
        compiler_params=_params(("parallel", "parallel")),
        name="adaln",
    )(c_all, _mx(ada_w), ada_b.reshape(depth, 1, n))


_C_Q, _C_CMP, _C_SK, _C_SV, _C_WIN, _C_GATE, _C_MQL, _C_MKV, _C_F1, _C_F2, _C_MOBA, _C_END = (
    0, 512, 768, 896, 1024, 1280, 1408, 1664, 1792, 1920, 2048, 2816)


def _inproj_kernel(x_ref, sh_ref, sc_ref, g_ref, wa_ref, qn_ref, wuq_ref, wuk_ref, kvn_ref, c4_ref, s4_ref,
                   oq, ocmp, osk, osv, owin, ogate, oqmla, olatkpe, omq, omk, omv):
    h = _modulate(x_ref[...], g_ref[...], sh_ref[...], sc_ref[...])
    y = _dot(h, wa_ref[...])
    oq[...] = y[:, _C_Q:_C_CMP] * NSA_SCALE
    ocmp[...] = y[:, _C_CMP:_C_SK]
    osk[...] = y[:, _C_SK:_C_SV]
    osv[...] = y[:, _C_SV:_C_WIN]
    owin[...] = y[:, _C_WIN:_C_GATE]
    ogate[...] = _sigmoid(y[:, _C_GATE:_C_MQL])
    c4 = c4_ref[...]
    s4 = s4_ref[...]
    olatkpe[:, 0:LANE] = _rms(y[:, _C_MKV:_C_F1], kvn_ref[...])
    olatkpe[:, LANE:2 * LANE] = y[:, _C_F1:_C_F2] * c4 + y[:, _C_F2:_C_MOBA] * s4
    mq = _dot(_rms(y[:, _C_MQL:_C_MKV], qn_ref[...]), wuq_ref[...])
    qpe = (mq[:, 256:384] * c4 + mq[:, 384:512] * s4) * MLA_SCALE
    qlat = _dot(mq[:, 0:256], wuk_ref[...]) * MLA_SCALE
    head_of_lane = _iota(qpe.shape, 1) >> _LOG2_ROPE
    for hd in range(MLA_HEADS):
        oqmla[:, hd * 256:hd * 256 + LANE] = qlat[:, hd * LANE:(hd + 1) * LANE]
        oqmla[:, hd * 256 + LANE:(hd + 1) * 256] = jnp.where(head_of_lane == hd, qpe, 0.0)
    omq[...] = y[:, _C_MOBA:_C_MOBA + 256] * MOBA_SCALE
    omk[...] = y[:, _C_MOBA + 256:_C_MOBA + 512]
    omv[...] = y[:, _C_MOBA + 512:_C_END]


_INPROJ_WIDTHS = (512, 256, 128, 128, 256, 128, 1024, 256, 256, 256, 256)


def inproj(x, mod_ops, norm_g, lw, c4, s4, table_spec, tm):
    t, d = x.shape
    (sh, sh_spec), (sc, sc_spec) = mod_ops[0], mod_ops[1]
    full = lambda a: pl.BlockSpec(a.shape, lambda i: (0,) * a.ndim)
    row = lambda w: pl.BlockSpec((tm, w), lambda i: (i, 0))
    return pl.pallas_call(
        _inproj_kernel,
        grid=(t // tm,),
        in_specs=[row(d), sh_spec, sc_spec, full(norm_g), full(lw["w_a"]), full(lw["qn"]), full(lw["w_uq"]),
                  full(lw["w_uk"]), full(lw["kvn"]), table_spec, table_spec],
        out_specs=[row(w) for w in _INPROJ_WIDTHS],
        out_shape=[jax.ShapeDtypeStruct((t, w), F32) for w in _INPROJ_WIDTHS],
        compiler_params=_params(("parallel",)),
        name="inproj",
    )(x, sh, sc, norm_g, lw["w_a"], lw["qn"], lw["w_uq"], lw["w_uk"], lw["kvn"], c4, s4)


def _cmp_a_core(x_refs, w_ref, o_ref, pp):
    cpp = PAGE_SIZE // NSA_CMP_STRIDE
    halves = []
    for kv, x_ref in enumerate(x_refs):
        acc = None
        for cp in range(NSA_CMP_STRIDE // 2):
            lhs = jnp.concatenate(
                [x_ref[:, pl.ds(2 * cp + k, cpp, stride=NSA_CMP_STRIDE), :].reshape(pp * cpp, LANE)
                 for k in range(2)], axis=1)
            part = _dot(lhs, w_ref[kv, cp])
            acc = part if acc is None else acc + part
        halves.append(acc)
    o_ref[...] = jnp.concatenate(halves, axis=1)


def _cmp_a_kernel(xa_ref, xb_ref, w_ref, o_ref, *, pp):
    _cmp_a_core((xa_ref, xb_ref), w_ref, o_ref, pp)


def _cmp_a_t_kernel(pt_ref, *refs, pp):
    x_refs = refs[:pp]
    w_ref, o_ref, xa_scr, xb_scr = refs[pp:]
    for p in range(pp):
        x = x_refs[p][...].T
        xa_scr[p] = x[:, 0:LANE]
        xb_scr[p] = x[:, LANE:2 * LANE]
    _cmp_a_core((xa_scr, xb_scr), w_ref, o_ref, pp)


def cmp_chunk_proj_t(pages_t, layer, page_list, w_chunks):
    _, _, width, ps = pages_t.shape
    n_used = page_list.shape[0]
    cpp = ps // NSA_CMP_STRIDE
    pp = _largest_divisor(n_used, CMP_PAGES_PAGED)
    nout = 2 * w_chunks.shape[-1]
    page = lambda k: pl.BlockSpec((None, None, width, ps), lambda i, pt: (layer, pt[i * pp + k], 0, 0))
    grid_spec = pltpu.PrefetchScalarGridSpec(
        num_scalar_prefetch=1,
        grid=(n_used // pp,),
        in_specs=[page(k) for k in range(pp)] + [pl.BlockSpec(w_chunks.shape, lambda i, pt: (0, 0, 0, 0))],
        out_specs=pl.BlockSpec((pp * cpp, nout), lambda i, pt: (i, 0)),
        scratch_shapes=[pltpu.VMEM((pp, ps, LANE), F32), pltpu.VMEM((pp, ps, LANE), F32)],
    )
    return pl.pallas_call(
        functools.partial(_cmp_a_t_kernel, pp=pp),
        grid_spec=grid_spec,
        out_shape=jax.ShapeDtypeStruct((n_used * cpp, nout), F32),
        compiler_params=_params(("parallel",)),
        name="cmp_chunk_proj_t",
    )(page_list, *([pages_t] * pp), w_chunks)


def _largest_divisor(n, cap):
    for p in range(min(cap, n), 0, -1):
        if n % p == 0:
            return p
    return 1


def cmp_chunk_proj(pages, layer, w_chunks):
    _, npg, ps, width = pages.shape
    cpp = ps // NSA_CMP_STRIDE
    pp = _largest_divisor(npg, CMP_PAGES)
    nout = 2 * w_chunks.shape[-1]
    return pl.pallas_call(
        functools.partial(_cmp_a_kernel, pp=pp),
        grid=(npg // pp,),
        in_specs=[pl.BlockSpec((None, pp, ps, LANE), lambda i: (layer, i, 0, 0)),
                  pl.BlockSpec((None, pp, ps, LANE), lambda i: (layer, i, 0, 1)),
                  pl.BlockSpec(w_chunks.shape, lambda i: (0, 0, 0, 0))],
        out_specs=pl.BlockSpec((pp * cpp, nout), lambda i: (i, 0)),
        out_shape=jax.ShapeDtypeStruct((npg * cpp, nout), F32),
        compiler_params=_params(("parallel",)),
        name="cmp_chunk_proj",
    )(pages, pages, w_chunks)


def _cmp_asm_kernel(a_ref, pe_ref, w2_ref, o_ref):
    a = a_ref[...]
    n = a.shape[0]
    hid_w = NSA_CMP_HID
    first = jnp.concatenate([a[:, k * 2 * hid_w:k * 2 * hid_w + hid_w] for k in range(2 * NSA_GROUPS)], axis=1)
    second = jnp.concatenate([a[:, k * 2 * hid_w + hid_w:(k + 1) * 2 * hid_w] for k in range(2 * NSA_GROUPS)], axis=1)
    hid = first + pltpu.roll(second, n - 1, 0) + pe_ref[...]
    o_ref[...] = _dot(_silu(hid), w2_ref[...])


def cmp_assemble(a_seq, pe_const, w2_bd):
    nb, n, wa = a_seq.shape
    return pl.pallas_call(
        _cmp_asm_kernel,
        grid=(nb,),
        in_specs=[pl.BlockSpec((None, n, wa), lambda b: (b, 0, 0)),
                  pl.BlockSpec(pe_const.shape, lambda b: (0, 0)),
                  pl.BlockSpec(w2_bd.shape, lambda b: (0, 0))],
        out_specs=pl.BlockSpec((None, n, w2_bd.shape[1]), lambda b: (b, 0, 0)),
        out_shape=jax.ShapeDtypeStruct((nb, n, w2_bd.shape[1]), F32),
        compiler_params=_params(("parallel",)),
        name="cmp_assemble",
    )(a_seq, pe_const, w2_bd)


def _nsa_cmp_kernel(q_ref, kcvc_ref, oc_ref, sel_ref, *, bb, tq, t0, nc, ns, n_top, own_in_range, slopes):
    tbase = t0 + pl.program_id(1) * tq
    ncp = kcvc_ref.shape[1]
    r = NSA_GROUPS * NSA_HPG * tq
    lane = _iota((tq, LANE), 1)
    t_rows = tbase + (_iota((r, 1), 0) & (tq - 1))
    cidx = _iota((1, ncp), 1)
    d_c = t_rows - (cidx * NSA_CMP_STRIDE + (NSA_CMP_LEN - 1))
    valid = (d_c >= 0) & (cidx < nc)
    d_cf = d_c.astype(F32)
    ci = _iota((ncp, SEL_LANES), 0) * NSA_CMP_STRIDE
    sj = _iota((ncp, SEL_LANES), 1) * NSA_SEL_BLOCK
    cover = ((ci < sj + NSA_SEL_BLOCK) & (ci + NSA_CMP_LEN > sj)
             & (_iota((ncp, SEL_LANES), 0) < nc) & (_iota((ncp, SEL_LANES), 1) < ns))
    cover = _mx(cover.astype(F32))
    halves = [(lane >= g * NSA_DH) & (lane < (g + 1) * NSA_DH) for g in range(NSA_GROUPS)]
    slope_col = jnp.concatenate([jnp.full((tq, 1), s, F32) for g in range(NSA_GROUPS) for s in slopes[g]], axis=0)
    imps = []
    for b in range(bb):
        kc = kcvc_ref[b, :, 0:LANE]
        vc = kcvc_ref[b, :, LANE:2 * LANE]
        rows = jnp.concatenate([jnp.where(halves[g], q_ref[b, :, c * LANE:(c + 1) * LANE], 0.0)
                                for g in range(NSA_GROUPS) for c in range(NSA_HPG)], axis=0)
        s = _dot3_nt(rows, kc) - slope_col * d_cf
        s = jnp.where(valid, s, NEG_INF)
        m = jnp.max(s, axis=-1, keepdims=True)
        m = jnp.where(m > NEG_INF, m, 0.0)
        e = jnp.where(valid, jnp.exp(s - m), 0.0)
        p = e / jnp.maximum(jnp.sum(e, axis=-1, keepdims=True), 1e-30)
        o = _dot(p, vc)
        for c in range(NSA_HPG):
            oc_ref[b, :, c * LANE:(c + 1) * LANE] = jnp.where(
                lane < NSA_DH, o[c * tq:(c + 1) * tq], o[(NSA_HPG + c) * tq:(NSA_HPG + c + 1) * tq])
        for g in range(NSA_GROUPS):
            psum = p[g * NSA_HPG * tq:(g * NSA_HPG + 1) * tq]
            for c in range(1, NSA_HPG):
                psum = psum + p[(g * NSA_HPG + c) * tq:(g * NSA_HPG + c + 1) * tq]
            ph, plo = _split2(psum)
            imps.append(jnp.dot(ph, cover, preferred_element_type=F32)
                        + jnp.dot(plo, cover, preferred_element_type=F32))
    imp = jnp.concatenate(imps, axis=0)
    n_sets = bb * NSA_GROUPS
    lane_all = _iota((n_sets * tq, SEL_LANES), 1)
    blk_q = (tbase + (_iota((n_sets * tq, 1), 0) & (tq - 1))) >> _LOG2_SEL
    sel = _topk_mask(imp, (lane_all < blk_q) & (lane_all < ns), n_top)
    if own_in_range:
        sel = jnp.where(lane_all == blk_q, 1.0, sel)
    for b in range(bb):
        for g in range(NSA_GROUPS):
            k = b * NSA_GROUPS + g
            sel_ref[b, :, g * SEL_LANES:(g + 1) * SEL_LANES] = sel[k * tq:(k + 1) * tq]


def nsa_cmp_select(q, kcvc, *, bb, tq, t0, nc, ns, own_in_range, slopes):
    nb, sq, qw = q.shape
    ncp = kcvc.shape[1]
    kern = functools.partial(_nsa_cmp_kernel, bb=bb, tq=tq, t0=t0, nc=nc, ns=ns, n_top=min(NSA_TOPN, ns),
                             own_in_range=own_in_range, slopes=slopes)
    return pl.pallas_call(
        kern,
        grid=(nb // bb, sq // tq),
        in_specs=[pl.BlockSpec((bb, tq, qw), lambda b, i: (b, i, 0)),
                  pl.BlockSpec((bb, ncp, kcvc.shape[2]), lambda b, i: (b, 0, 0))],
        out_specs=[pl.BlockSpec((bb, tq, qw), lambda b, i: (b, i, 0)),
                   pl.BlockSpec((bb, tq, NSA_GROUPS * SEL_LANES), lambda b, i: (b, i, 0))],
        out_shape=[jax.ShapeDtypeStruct((nb, sq, qw), F32),
                   jax.ShapeDtypeStruct((nb, sq, NSA_GROUPS * SEL_LANES), F32)],
        compiler_params=_params(("parallel", "parallel")),
        name="nsa_cmp_select",
    )(q, kcvc)


def _block_mean_kernel(k_ref, o_ref, *, nblk):
    o_ref[...] = jnp.zeros(o_ref.shape, F32)
    for j in range(nblk):
        blk = k_ref[j * MOBA_BLOCK:(j + 1) * MOBA_BLOCK, :]
        o_ref[j:j + 1, :] = jnp.sum(blk, axis=0, keepdims=True) * (1.0 / MOBA_BLOCK)


def moba_block_mean(k):
    nb, s, w = k.shape
    nblk = s // MOBA_BLOCK
    return pl.pallas_call(
        functools.partial(_block_mean_kernel, nblk=nblk),
        grid=(nb,),
        in_specs=[pl.BlockSpec((None, s, w), lambda b: (b, 0, 0))],
        out_specs=pl.BlockSpec((None, SEL_LANES, w), lambda b: (b, 0, 0)),
        out_shape=jax.ShapeDtypeStruct((nb, SEL_LANES, w), F32),
        compiler_params=_params(("parallel",)),
        name="moba_block_mean",
    )(k)


def _kmean_paged_kernel(pt_ref, *refs, nblk, ppb):
    x_refs, o_ref = refs[:-1], refs[-1]
    w = o_ref.shape[1]
    ones = jnp.ones((SUBLANE, x_refs[0].shape[1]), MXU_DTYPE)
    f = lambda a: lax.dot_general(ones, a, _NT, preferred_element_type=F32)
    o_ref[...] = jnp.zeros(o_ref.shape, F32)
    for j in range(nblk):
        x = jnp.concatenate([x_refs[j * ppb + k][...] for k in range(ppb)], axis=0)
        h1, h2 = _split2(x)
        sums = (f(h1) + f(h2))[0:1]
        acc = sums[:, 0:w]
        for k in range(1, ppb):
            acc = acc + sums[:, k * w:(k + 1) * w]
        o_ref[j:j + 1, :] = acc * (1.0 / MOBA_BLOCK)


def moba_kmean_paged(pages_t, layer, page_table):
    nb, npg = page_table.shape
    _, _, w, ps = pages_t.shape
    ppb = MOBA_BLOCK // ps
    page = lambda k: pl.BlockSpec((None, None, w, ps), lambda b, pt: (layer, pt[b, k], 0, 0))
    grid_spec = pltpu.PrefetchScalarGridSpec(
        num_scalar_prefetch=1,
        grid=(nb,),
        in_specs=[page(k) for k in range(npg)],
        out_specs=pl.BlockSpec((None, SEL_LANES, w), lambda b, pt: (b, 0, 0)),
    )
    return pl.pallas_call(
        functools.partial(_kmean_paged_kernel, nblk=npg // ppb, ppb=ppb),
        grid_spec=grid_spec,
        out_shape=jax.ShapeDtypeStruct((nb, SEL_LANES, w), F32),
        compiler_params=_params(("parallel",)),
        name="moba_kmean_paged",
    )(page_table, *([pages_t] * npg))


def _moba_select_kernel(q_ref, km_ref, sel_ref, *, tq, t0, n_full, n_sel, own_in_range):
    tbase = t0 + pl.program_id(1) * tq
    q = q_ref[...]
    km = km_ref[...]
    lane_q = _iota(q.shape, 1)
    lane = _iota((tq, SEL_LANES), 1)
    blk_q = (tbase + _iota((tq, 1), 0)) >> _LOG2_MOBA
    cand = (lane < blk_q) & (lane < n_full)
    for hd in range(MOBA_HEADS):
        qh = jnp.where((lane_q >> _LOG2_DH) == hd, q, 0.0)
        gs = _dot3_nt(qh, km)
        sel = _topk_mask(gs, cand, n_sel)
        if own_in_range:
            sel = jnp.where(lane == blk_q, 1.0, sel)
        sel_ref[:, hd * SEL_LANES:(hd + 1) * SEL_LANES] = sel


def moba_select(q, kmean, *, tq, t0, n_full, own_in_range):
    nb, sq, qw = q.shape
    kern = functools.partial(_moba_select_kernel, tq=tq, t0=t0, n_full=n_full,
                             n_sel=min(MOBA_TOPK, n_full), own_in_range=own_in_range)
    return pl.pallas_call(
        kern,
        grid=(nb, sq // tq),
        in_specs=[pl.BlockSpec((None, tq, qw), lambda b, i: (b, i, 0)),
                  pl.BlockSpec((None,) + kmean.shape[1:], lambda b, i: (b, 0, 0))],
        out_specs=pl.BlockSpec((None, tq, MOBA_HEADS * SEL_LANES), lambda b, i: (b, i, 0)),
        out_shape=jax.ShapeDtypeStruct((nb, sq, MOBA_HEADS * SEL_LANES), F32),
        compiler_params=_params(("parallel", "parallel")),
        name="moba_select",
    )(q, kmean)


def _nsa_jobs(slopes):
    heads = [(g, c) for g in range(NSA_GROUPS) for c in range(NSA_HPG)]
    return [dict(qcols=[(c * LANE, LANE) for g, c in heads],
                 qmasks=[(g * NSA_DH, (g + 1) * NSA_DH) for g, c in heads], q2cols=None,
                 kcol=(0, LANE), vcol=(0, LANE), slopes=[slopes[g][c] for g, c in heads],
                 sels=[g * SEL_LANES for g, c in heads], sel_shift=_LOG2_SEL,
                 outs=[(0, c * LANE, g * NSA_DH, (g + 1) * NSA_DH) for g, c in heads])]


def _moba_jobs(slopes):
    hs = range(MOBA_HEADS)
    width = MOBA_HEADS * MOBA_DH
    return [dict(qcols=[(0, width) for _ in hs], qmasks=[(h * MOBA_DH, (h + 1) * MOBA_DH) for h in hs],
                 q2cols=None, kcol=(0, width), vcol=(0, width), slopes=[slopes[h] for h in hs],
                 sels=[h * SEL_LANES for h in hs], sel_shift=_LOG2_MOBA,
                 outs=[((h // 2) * LANE, (h // 2) * LANE, (h % 2) * MOBA_DH, (h % 2 + 1) * MOBA_DH) for h in hs])]


def _stack_q(q_ref, job, tq):
    parts = []
    for (off, w), qmask in zip(job["qcols"], job["qmasks"]):
        x = q_ref[:, off:off + w]
        if qmask is not None:
            lo, hi = qmask
            ln = _iota(x.shape, 1)
            x = jnp.where((ln >= lo) & (ln < hi), x, 0.0)
        parts.append(x)
    qs = _mx(jnp.concatenate(parts, axis=0))
    q2 = None
    if job["q2cols"] is not None:
        q2 = _mx(jnp.concatenate([q_ref[:, off:off + w] for off, w in job["q2cols"]], axis=0))
    slope_col = None
    if job["slopes"] is not None:
        slope_col = jnp.concatenate([jnp.full((tq, 1), s, F32) for s in job["slopes"]], axis=0)
    return qs, q2, slope_col


def _block_mask(sel_ref, job, k0, span):
    shift = job["sel_shift"]
    expand = _mx((_iota((SEL_LANES, span), 0) == ((k0 + _iota((SEL_LANES, span), 1)) >> shift)).astype(F32))
    ems = {}
    for soff in job["sels"]:
        if soff not in ems:
            ems[soff] = jnp.dot(_mx(sel_ref[:, soff:soff + SEL_LANES]), expand, preferred_element_type=F32)
    return jnp.concatenate([ems[soff] for soff in job["sels"]], axis=0)


def _softmax_step(s, allowed, v, m, l, acc, v_t=False):
    if allowed is not None:
        s = jnp.where(allowed, s, NEG_INF)
    m_new = jnp.maximum(m, jnp.max(s, axis=-1, keepdims=True))
    m_safe = jnp.where(m_new > NEG_INF, m_new, 0.0)
    alpha = jnp.exp(m - m_safe)
    p = jnp.exp(s - m_safe)
    l = alpha * l + jnp.sum(p, axis=-1, keepdims=True)
    acc = alpha * acc + (_dot_nt(p, v) if v_t else _dot(p, v))
    return m_new, l, acc


def _write_outputs(o_ref, jobs, results, tq):
    lane = _iota((tq, LANE), 1)
    cols = {}
    for job, res in zip(jobs, results):
        for k, (voff, off, lo, hi) in enumerate(job["outs"]):
            piece = res[k * tq:(k + 1) * tq, voff:voff + LANE]
            if (lo, hi) == (0, LANE):
                cols[off] = piece
            else:
                prev = cols.get(off, jnp.zeros((tq, LANE), F32))
                cols[off] = jnp.where((lane >= lo) & (lane < hi), piece, prev)
    for off, val in cols.items():
        o_ref[:, off:off + LANE] = val


def _flash_kernel(*refs, jobs, tq, tk, q_pos0, window, has_sel, has_v):
    refs = list(refs)
    q_ref = refs.pop(0)
    k_ref = refs.pop(0)
    v_ref = refs.pop(0) if has_v else k_ref
    sel_ref = refs.pop(0) if has_sel else None
    o_ref = refs.pop(0)
    t_lo = q_pos0 + pl.program_id(1) * tq
    n_kt = k_ref.shape[0] // tk
    kt_hi = jnp.minimum((t_lo + tq - 1) // tk + 1, n_kt)
    kt_lo = jnp.maximum(t_lo - window + 1, 0) // tk if window else 0
    results = []
    for job in jobs:
        qs, q2, slope_col = _stack_q(q_ref, job, tq)
        nq = len(job["qcols"])
        r = nq * tq
        t_rows = t_lo + (_iota((r, 1), 0) & (tq - 1))
        koff, kw = job["kcol"]
        voff, vw = job["vcol"]
        masked = has_sel and job["sels"] is not None
        t_win = t_rows - window

        def body(kt, carry, causal, qs=qs, slope_col=slope_col, t_rows=t_rows, t_win=t_win, koff=koff, kw=kw,
                 voff=voff, vw=vw, masked=masked, job=job):
            m, l, acc = carry
            k0 = pl.multiple_of(kt * tk, tk)
            ktile = k_ref[pl.ds(k0, tk), koff:koff + kw]
            vtile = v_ref[pl.ds(k0, tk), voff:voff + vw]
            s = _dot_nt(qs, ktile)
            kpos = k0 + _iota((1, tk), 1)
            allowed = None
            if causal:
                allowed = t_rows >= kpos
                if window:
                    allowed = allowed & (t_win < kpos)
            if slope_col is not None:
                s = s - slope_col * (t_lo - kpos).astype(F32)
            if masked:
                picked = _block_mask(sel_ref, job, k0, tk) > 0.5
                allowed = picked if allowed is None else allowed & picked
            return _softmax_step(s, allowed, vtile, m, l, acc)

        init = (jnp.full((r, 1), NEG_INF, F32), jnp.zeros((r, 1), F32), jnp.zeros((r, vw), F32))
        if window:
            m, l, acc = lax.fori_loop(kt_lo, kt_hi, functools.partial(body, causal=True), init)
        else:
            kt_mid = jnp.clip((t_lo + 1) // tk, kt_lo, kt_hi)
            carry = lax.fori_loop(kt_lo, kt_mid, functools.partial(body, causal=False), init)
            m, l, acc = lax.fori_loop(kt_mid, kt_hi, functools.partial(body, causal=True), carry)
        results.append(acc / jnp.maximum(l, 1e-30))
    _write_outputs(o_ref, jobs, results, tq)


def flash_attention(q, k, v, sel, *, jobs, out_width, tq, tk, q_pos0=0, window=0):
    nb, sq, qw = q.shape
    sk = k.shape[1]
    ops = [q, k]
    specs = [pl.BlockSpec((None, tq, qw), lambda b, i: (b, i, 0)),
             pl.BlockSpec((None, sk, k.shape[2]), lambda b, i: (b, 0, 0))]
    if v is not None:
        ops.append(v)
        specs.append(pl.BlockSpec((None, sk, v.shape[2]), lambda b, i: (b, 0, 0)))
    if sel is not None:
        ops.append(sel)
        specs.append(pl.BlockSpec((None, tq, sel.shape[2]), lambda b, i: (b, i, 0)))
    kern = functools.partial(_flash_kernel, jobs=jobs, tq=tq, tk=tk, q_pos0=q_pos0, window=window,
                             has_sel=sel is not None, has_v=v is not None)
    return pl.pallas_call(
        kern,
        grid=(nb, sq // tq),
        in_specs=specs,
        out_specs=pl.BlockSpec((None, tq, out_width), lambda b, i: (b, i, 0)),
        out_shape=jax.ShapeDtypeStruct((nb, sq, out_width), F32),
        compiler_params=_params(("parallel", "parallel")),
        name="flash_attention",
    )(*ops)


def _paged_kernel(pt_ref, *refs, jobs, npages, page_len, past_len, key_pos0, window, n_new, has_sel, has_k2,
                  has_v, k_t, k2_t, v_t):
    refs = list(refs)
    q_ref = refs.pop(0)
    sel_ref = refs.pop(0) if has_sel else None
    knew_ref = refs.pop(0)
    k2new_ref = refs.pop(0) if has_k2 else None
    vnew_ref = refs.pop(0) if has_v else knew_ref
    k_refs = [refs.pop(0) for _ in range(npages)]
    k2_refs = [refs.pop(0) for _ in range(npages)] if has_k2 else None
    v_refs = [refs.pop(0) for _ in range(npages)] if has_v else k_refs
    o_ref = refs.pop(0)
    m_scr, l_scr, acc_scr = refs
    j = pl.program_id(1)
    tq = q_ref.shape[0]
    span = npages * page_len

    @pl.when(j == 0)
    def _():
        m_scr[...] = jnp.full(m_scr.shape, NEG_INF, F32)
        l_scr[...] = jnp.zeros(l_scr.shape, F32)
        acc_scr[...] = jnp.zeros(acc_scr.shape, F32)

    cat = lambda rs, t: jnp.concatenate([r[...] for r in rs], axis=1 if t else 0)
    kcat = cat(k_refs, k_t)
    vcat = cat(v_refs, v_t) if has_v else kcat
    k2cat = _mx(cat(k2_refs, k2_t)) if has_k2 else None
    k0 = j * span
    kpos = key_pos0 + k0 + _iota((1, span), 1)
    row0 = 0
    stacked = []
    for job in jobs:
        qs, q2, slope_col = _stack_q(q_ref, job, tq)
        nq = len(job["qcols"])
        r = nq * tq
        t_rows = past_len + (_iota((r, 1), 0) & (tq - 1))
        koff, kw = job["kcol"]
        voff, vw = job["vcol"]
        s = _dot(qs, kcat[koff:koff + kw, :]) if k_t else _dot_nt(qs, kcat[:, koff:koff + kw])
        if q2 is not None:
            s = s + (jnp.dot(q2, k2cat, preferred_element_type=F32) if k2_t
                     else lax.dot_general(q2, k2cat, _NT, preferred_element_type=F32))
        allowed = t_rows >= kpos
        if window:
            allowed = allowed & ((t_rows - window) < kpos)
        if slope_col is not None:
            s = s - slope_col * (past_len - kpos).astype(F32)
        if has_sel and job["sels"] is not None:
            allowed = allowed & (_block_mask(sel_ref, job, k0, span) > 0.5)
        rows = slice(row0, row0 + r)
        vsl = vcat[voff:voff + vw, :] if v_t else vcat[:, voff:voff + vw]
        m, l, acc = _softmax_step(s, allowed, vsl, m_scr[rows], l_scr[rows], acc_scr[rows], v_t)
        m_scr[rows] = m
        l_scr[rows] = l
        acc_scr[rows] = acc
        stacked.append((qs, q2, slope_col, t_rows, rows, r))
        row0 += r

    @pl.when(j == pl.num_programs(1) - 1)
    def _():
        results = []
        n_rows = knew_ref.shape[0]
        cpos = _iota((1, n_rows), 1)
        for job, (qs, q2, slope_col, t_rows, rows, r) in zip(jobs, stacked):
            koff, kw = job["kcol"]
            voff, vw = job["vcol"]
            s = _dot_nt(qs, knew_ref[:, koff:koff + kw])
            if q2 is not None:
                s = s + _dot_nt(q2, k2new_ref[...])
            allowed = (t_rows >= past_len + cpos) & (cpos < n_new)
            if slope_col is not None:
                s = s + slope_col * cpos.astype(F32)
            m, l, acc = _softmax_step(s, allowed, vnew_ref[:, voff:voff + vw],
                                      m_scr[rows], l_scr[rows], acc_scr[rows])
            results.append(acc / jnp.maximum(l, 1e-30))
        _write_outputs(o_ref, jobs, results, tq)


def paged_attention(q, sel, k_new, k2_new, v_new, k_pool, k2_pool, v_pool, page_table, layer, *,
                    jobs, out_width, past_len, n_new, k_t, k2_t=False, v_t=False, key_pos0=0, window=0,
                    pages_per_step=DEC_PAGES):
    nb, tq, qw = q.shape
    npg = page_table.shape[1]
    p = _largest_divisor(npg, pages_per_step)
    page_len = k_pool.shape[3] if k_t else k_pool.shape[2]
    if v_pool is None:
        v_t = k_t
    seq = lambda a: pl.BlockSpec((None,) + a.shape[1:], lambda b, j, pt: (b, 0, 0))
    page = lambda pool, k: pl.BlockSpec((None, None) + pool.shape[2:],
                                        lambda b, j, pt: (layer, pt[b, j * p + k], 0, 0))
    ops, specs = [q], [seq(q)]
    if sel is not None:
        ops.append(sel)
        specs.append(seq(sel))
    ops.append(k_new)
    specs.append(seq(k_new))
    if k2_new is not None:
        ops.append(k2_new)
        specs.append(seq(k2_new))
    if v_new is not None:
        ops.append(v_new)
        specs.append(seq(v_new))
    for pool in (k_pool, k2_pool, v_pool):
        if pool is not None:
            ops += [pool] * p
            specs += [page(pool, k) for k in range(p)]
    rows = sum(len(job["qcols"]) for job in jobs) * tq
    acc_w = max(job["vcol"][1] for job in jobs)
    grid_spec = pltpu.PrefetchScalarGridSpec(
        num_scalar_prefetch=1,
        grid=(nb, npg // p),
        in_specs=specs,
        out_specs=pl.BlockSpec((None, tq, out_width), lambda b, j, pt: (b, 0, 0)),
        scratch_shapes=[pltpu.VMEM((rows, 1), F32), pltpu.VMEM((rows, 1), F32), pltpu.VMEM((rows, acc_w), F32)],
    )
    kern = functools.partial(_paged_kernel, jobs=jobs, npages=p, page_len=page_len, past_len=past_len,
                             key_pos0=key_pos0, window=window, n_new=n_new, has_sel=sel is not None,
                             has_k2=k2_pool is not None, has_v=v_pool is not None, k_t=k_t, k2_t=k2_t, v_t=v_t)
    return pl.pallas_call(
        kern,
        grid_spec=grid_spec,
        out_shape=jax.ShapeDtypeStruct((nb, tq, out_width), F32),
        compiler_params=_params(("parallel", "arbitrary")),
        name="paged_attention",
    )(page_table, *ops)


def _merge_kernel(x_ref, sh_ref, sc_ref, gt_ref, g_ref, oc_ref, os_ref, ow_ref, gate_ref, ctx_ref, om_ref,
                  wmg_ref, wa_ref, wuv_ref, wb_ref, wc_ref, wo_ref, o_ref):
    x = x_ref[...]
    h = _modulate(x, g_ref[...], sh_ref[...], sc_ref[...])
    mg = _sigmoid(_dot(h, wmg_ref[...]))
    n_chunk = NSA_HEADS
    width = n_chunk * NSA_DH
    rowi = _iota((LANE, N_BRANCH * width), 0)
    coli = _iota((LANE, N_BRANCH * width), 1)
    expand = _mx((rowi == (coli >> _LOG2_DH)).astype(F32))
    ge = _dot_exact01(gate_ref[...], expand)
    oa = ge[:, 0:width] * oc_ref[...] + ge[:, width:2 * width] * os_ref[...] + ge[:, 2 * width:] * ow_ref[...]
    ob = _dot(ctx_ref[...], wuv_ref[...])
    d = x.shape[1]
    mixed = (mg[:, 0:d] * _dot(oa, wa_ref[...]) + mg[:, d:2 * d] * _dot(ob, wb_ref[...])
             + mg[:, 2 * d:] * _dot(om_ref[...], wc_ref[...]))
    o_ref[...] = x + gt_ref[...] * _dot(mixed, wo_ref[...])


def merge(x, mod_ops, norm_g, oc, os_, ow, gate, ctx, om, lw, tm):
    t, d = x.shape
    full = lambda a: pl.BlockSpec(a.shape, lambda i: (0,) * a.ndim)
    row = lambda a: pl.BlockSpec((tm, a.shape[1]), lambda i: (i, 0))
    (sh, sh_s), (sc, sc_s), (gt, gt_s) = mod_ops[0], mod_ops[1], mod_ops[2]
    ws = [lw["w_mg"], lw["w_br_nsa"], lw["w_uv"], lw["w_br_mla"], lw["w_br_moba"], lw["w_out"]]
    acts = [oc, os_, ow, gate, ctx, om]
    return pl.pallas_call(
        _merge_kernel,
        grid=(t // tm,),
        in_specs=[row(x), sh_s, sc_s, gt_s, full(norm_g)] + [row(a) for a in acts] + [full(w) for w in ws],
        out_specs=row(x),
        out_shape=jax.ShapeDtypeStruct((t, d), F32),
        compiler_params=_params(("parallel",)),
        name="merge",
    )(x, sh, sc, gt, norm_g, *acts, *ws)


def _ffn_kernel(x_ref, sh_ref, sc_ref, gt_ref, g_ref, wi_ref, wo_ref, fn_ref, o_ref, *, final):
    x = x_ref[...]
    h = _modulate(x, g_ref[...], sh_ref[...], sc_ref[...])
    au = _dot(h, wi_ref[...])
    hid = au.shape[1] // 2
    y = x + gt_ref[...] * _dot(_silu(au[:, :hid]) * au[:, hid:], wo_ref[...])
    o_ref[...] = _rms(y, fn_ref[...]) if final else y


def ffn(x, mod_ops, norm_g, lw, final_norm, final, tm):
    t, d = x.shape
    full = lambda a: pl.BlockSpec(a.shape, lambda i: (0,) * a.ndim, pipeline_mode=pl.Buffered(1))
    row = pl.BlockSpec((tm, d), lambda i: (i, 0))
    (sh, sh_s), (sc, sc_s), (gt, gt_s) = mod_ops[3], mod_ops[4], mod_ops[5]
    return pl.pallas_call(
        functools.partial(_ffn_kernel, final=final),
        grid=(t // tm,),
        in_specs=[row, sh_s, sc_s, gt_s, full(norm_g), full(lw["ffn_in"]), full(lw["ffn_out"]), full(final_norm)],
        out_specs=row,
        out_shape=jax.ShapeDtypeStruct((t, d), F32),
        compiler_params=_params(("parallel",)),
        name="ffn",
    )(x, sh, sc, gt, norm_g, lw["ffn_in"], lw["ffn_out"], final_norm)


def _prep_layer(l, w_in, mla_q_norm, mla_w_uq, mla_kv_norm, mla_w_uk, mla_w_uv, nsa_cmp_w1, nsa_cmp_pe,
                nsa_cmp_w2, w_br_nsa, w_br_mla, w_br_moba, w_out, ffn_w_in, ffn_w_out):
    d = w_in.shape[1]
    g, hpg, dh = NSA_GROUPS, NSA_HPG, NSA_DH
    cuts = np.cumsum(IN_WIDTHS)[:-1].tolist()
    nq, nkv, ng, mql, mkvl, mkpe, mqkv, mg = jnp.split(w_in[l], cuts, axis=1)
    nq_p = nq.reshape(d, g, hpg, dh).transpose(0, 2, 1, 3).reshape(d, g * hpg * dh)
    ng_p = ng.reshape(d, g, hpg, N_BRANCH).transpose(0, 3, 2, 1).reshape(d, N_BRANCH * g * hpg)
    ng_p = jnp.pad(ng_p, ((0, 0), (0, LANE - ng_p.shape[1])))
    half = MLA_ROPE // 2
    swap = lambda a: jnp.concatenate([a[..., half:], a[..., :half]], axis=-1)
    rep = LANE // MLA_ROPE
    f1 = jnp.tile(mkpe, (1, rep))
    f2 = jnp.tile(swap(mkpe), (1, rep))
    w_a = jnp.concatenate([nq_p, nkv, ng_p, mql, mkvl, f1, f2, mqkv], axis=1)
    uq = mla_w_uq[l]
    uq_nope = uq[:, :, :MLA_NOPE].reshape(MLA_Q_LORA, MLA_HEADS * MLA_NOPE)
    uq_pe = uq[:, :, MLA_NOPE:]
    w_uq = jnp.concatenate([uq_nope, uq_pe.reshape(MLA_Q_LORA, -1), swap(uq_pe).reshape(MLA_Q_LORA, -1)], axis=1)
    uk = mla_w_uk[l]
    w_uk = block_diag(*[uk[:, hd, :].T for hd in range(MLA_HEADS)])
    uv = mla_w_uv[l]
    w_uv = block_diag(*[uv[:, hd, :] for hd in range(MLA_HEADS)])
    w1 = nsa_cmp_w1[l]
    hid = w1.shape[-1]
    half_c = NSA_CMP_STRIDE // 2
    base = w1.reshape(2, NSA_CMP_R, half_c, 2, dh, hid).transpose(0, 2, 3, 4, 1, 5)
    wc = jnp.zeros((2, half_c, 2, g, dh, g, NSA_CMP_R, hid), F32)
    for gi in range(g):
        wc = wc.at[:, :, :, gi, :, gi, :, :].set(base)
    w_chunks = wc.reshape(2, half_c, 2 * g * dh, g * NSA_CMP_R * hid)
    pe = nsa_cmp_pe[l].reshape(2, NSA_CMP_R, NSA_CMP_STRIDE, dh)
    pe_h = jnp.einsum('krcd,krcdh->kh', pe, w1, precision=lax.Precision.HIGHEST)
    pe_const = jnp.concatenate([pe_h[kv] for kv in range(2) for _ in range(g)])[None, :]
    w2 = nsa_cmp_w2[l]
    w2_bd = block_diag(*[w2[kv] for kv in range(2) for _ in range(g)])
    br_nsa = w_br_nsa[l].reshape(g, hpg, dh, d).transpose(1, 0, 2, 3).reshape(g * hpg * dh, d)
    return dict(
        w_a=_mx(w_a), qn=mla_q_norm[l][None, :], w_uq=_mx(w_uq), w_uk=_mx(w_uk), kvn=mla_kv_norm[l][None, :],
        w_chunks=_mx(w_chunks), pe_const=pe_const, w2_bd=_mx(w2_bd), w_mg=_mx(mg), w_br_nsa=_mx(br_nsa),
        w_uv=_mx(w_uv), w_br_mla=_mx(w_br_mla[l]), w_br_moba=_mx(w_br_moba[l]), w_out=_mx(w_out[l]),
        ffn_in=_mx(ffn_w_in[l]), ffn_out=_mx(ffn_w_out[l]))


def _rope_tables(pos):
    half = MLA_ROPE // 2
    inv = ROPE_THETA ** (-jnp.arange(half, dtype=F32) / half)
    ang = pos.astype(F32)[:, None] * inv
    cos, sin = jnp.cos(ang), jnp.sin(ang)
    rep = LANE // MLA_ROPE
    return (jnp.tile(jnp.concatenate([cos, cos], axis=1), (1, rep)),
            jnp.tile(jnp.concatenate([-sin, sin], axis=1), (1, rep)))


def _pad_rows(a, rows):
    return jnp.pad(a, ((0, 0), (0, rows - a.shape[1]), (0, 0)))


def kernel(x_prompt, x_sample, c_prompt, c_sample, cache_nsa_cmp_kv, cache_nsa_slc_k, cache_nsa_slc_v, state_nsa_win_kv, cache_mla_latent, cache_mla_kpe, cache_moba_k, cache_moba_v, page_table, ada_w, ada_b, norm_mix, norm_ffn, w_in, nsa_cmp_w1, nsa_cmp_pe, nsa_cmp_w2, mla_q_norm, mla_w_uq, mla_kv_norm, mla_w_uk, mla_w_uv, w_br_nsa, w_br_mla, w_br_moba, w_out, ffn_w_in, ffn_w_out, final_norm):
    depth = w_in.shape[0]
    nb, seq, d = x_prompt.shape
    db, ds, _ = x_sample.shape
    npg = page_table.shape[1]
    past_len = npg * PAGE_SIZE
    win_len = state_nsa_win_kv.shape[2]
    tp, ts = nb * seq, db * ds
    tm = TOKEN_TILE
    tm_s = min(TOKEN_TILE, ts)
    assert seq % MOBA_BLOCK == 0 and seq % ATT_TK_LONG == 0 and seq % (2 * ATT_TQ) == 0 and seq % tm == 0 and ts % tm_s == 0 and tm_s % SUBLANE == 0
    assert past_len % MOBA_BLOCK == 0 and ds <= DEC_Q and win_len == NSA_WINDOW <= past_len
    assert past_len // NSA_SEL_BLOCK <= SEL_LANES and seq // NSA_SEL_BLOCK <= SEL_LANES
    nsa_sl, moba_sl = alibi_slopes()
    nsa_jobs, moba_jobs = _nsa_jobs(nsa_sl), _moba_jobs(moba_sl)
    win_jobs = [dict(j, sels=None, vcol=(LANE, LANE)) for j in nsa_jobs]
    mla_jobs = [dict(qcols=[(hd * 256, 256) for hd in range(MLA_HEADS)], qmasks=[None] * MLA_HEADS, q2cols=None,
                     kcol=(0, 256), vcol=(0, LANE), slopes=None, sels=None, sel_shift=None,
                     outs=[(0, hd * LANE, 0, LANE) for hd in range(MLA_HEADS)])]
    mla_dec_jobs = [dict(mla_jobs[0], qcols=[(hd * 256, LANE) for hd in range(MLA_HEADS)], kcol=(0, LANE),
                         q2cols=[(hd * 256 + LANE + hd * MLA_ROPE, MLA_ROPE) for hd in range(MLA_HEADS)])]

    n_c = nb + db
    c_all = jnp.pad(jnp.concatenate([c_prompt, c_sample], axis=0), ((0, -n_c % SUBLANE), (0, 0)))
    mod_all = adaln_all(c_all, ada_w, ada_b)

    def mod_ops_prompt(l):
        arr = mod_all[l, :nb].reshape(nb * N_MOD, 1, d)
        tiles = seq // tm
        return [(arr, pl.BlockSpec((None, 1, d), lambda i, k=k: ((i // tiles) * N_MOD + k, 0, 0)))
                for k in range(N_MOD)]

    def mod_ops_sample(l):
        arr = jnp.repeat(mod_all[l, nb:n_c].reshape(db, N_MOD, d), ds, axis=0).transpose(1, 0, 2)
        return [(arr, pl.BlockSpec((None, tm_s, d), lambda i, k=k: (k, i, 0))) for k in range(N_MOD)]

    c4p, s4p = _rope_tables(jnp.arange(seq))
    c4s, s4s = _rope_tables(jnp.tile(past_len + jnp.arange(ds), db))
    tiles_p = seq // tm
    tbl_spec_p = pl.BlockSpec((tm, LANE), lambda i: (i % tiles_p, 0))
    tbl_spec_s = pl.BlockSpec((tm_s, LANE), lambda i: (i, 0))

    def pages_t(a):
        nd = a.ndim
        return jnp.transpose(a, (0, 1) + tuple(range(3, nd)) + (2,)).reshape(a.shape[0], a.shape[1], -1, a.shape[2])

    cmp_pool = pages_t(cache_nsa_cmp_kv)
    slc_k_pool, slc_v_pool = pages_t(cache_nsa_slc_k), pages_t(cache_nsa_slc_v)
    moba_k_pool, moba_v_pool = pages_t(cache_moba_k), pages_t(cache_moba_v)
    kpe_pool = pages_t(cache_mla_kpe)
    win_pool = pages_t(state_nsa_win_kv)
    ident_win = jnp.arange(db, dtype=jnp.int32).reshape(db, 1)

    xp = x_prompt.reshape(tp, d)
    xs = x_sample.reshape(ts, d)
    st_p, st_s = [], []
    for l in range(depth):
        lw = _prep_layer(l, w_in, mla_q_norm, mla_w_uq, mla_kv_norm, mla_w_uk, mla_w_uv, nsa_cmp_w1, nsa_cmp_pe,
                         nsa_cmp_w2, w_br_nsa, w_br_mla, w_br_moba, w_out, ffn_w_in, ffn_w_out)
        g_mix, g_ffn = norm_mix[l][None, :], norm_ffn[l][None, :]
        last = l == depth - 1
        fin = final_norm[None, :]

        mods = mod_ops_prompt(l)
        (q, cmp, sk, sv, win, gate, qmla, latkpe, mq, mk, mv) = inproj(xp, mods, g_mix, lw, c4p, s4p, tbl_spec_p, tm)
        b3 = lambda a: a.reshape(nb, seq, a.shape[-1])
        a_chunks = cmp_chunk_proj(cmp.reshape(1, tp // PAGE_SIZE, PAGE_SIZE, cmp.shape[-1]), 0, lw["w_chunks"])
        kcvc = cmp_assemble(a_chunks.reshape(nb, seq // NSA_CMP_STRIDE, -1), lw["pe_const"], lw["w2_bd"])
        oc, sel = nsa_cmp_select(b3(q), kcvc, bb=1, tq=ATT_TQ, t0=0, nc=seq // NSA_CMP_STRIDE - 1,
                                 ns=seq // NSA_SEL_BLOCK, own_in_range=True, slopes=nsa_sl)
        os_ = flash_attention(b3(q), b3(sk), b3(sv), sel, jobs=nsa_jobs, out_width=512, tq=ATT_TQ, tk=ATT_TK_LONG)
        ow = flash_attention(b3(q), b3(win), None, None, jobs=win_jobs, out_width=512, tq=ATT_TQ, tk=ATT_TK,
                             window=NSA_WINDOW)
        ctx = flash_attention(b3(qmla), b3(latkpe), None, None, jobs=mla_jobs, out_width=512, tq=2 * ATT_TQ,
                              tk=ATT_TK_LONG)
        msel = moba_select(b3(mq), moba_block_mean(b3(mk)), tq=ATT_TQ, t0=0, n_full=seq // MOBA_BLOCK,
                           own_in_range=True)
        om = flash_attention(b3(mq), b3(mk), b3(mv), msel, jobs=moba_jobs, out_width=256, tq=2 * ATT_TQ,
                             tk=ATT_TK_LONG)
        f2 = lambda a: a.reshape(tp, a.shape[-1])
        x1 = merge(xp, mods, g_mix, f2(oc), f2(os_), f2(ow), gate, f2(ctx), f2(om), lw, tm)
        xp = ffn(x1, mods, g_ffn, lw, fin, last, tm)
        win_keep = min(NSA_WINDOW, seq)
        st_p.append((cmp.reshape(nb, seq, 2, NSA_GROUPS, NSA_DH), sk.reshape(nb, seq, NSA_GROUPS, NSA_DH),
                     sv.reshape(nb, seq, NSA_GROUPS, NSA_DH),
                     b3(win)[:, seq - win_keep:].reshape(nb, win_keep, 2, NSA_GROUPS, NSA_DH),
                     b3(latkpe)[:, :, :MLA_KV_LORA], b3(latkpe)[:, :, LANE:LANE + MLA_ROPE],
                     mk.reshape(nb, seq, MOBA_HEADS, MOBA_DH), mv.reshape(nb, seq, MOBA_HEADS, MOBA_DH)))

        mods = mod_ops_sample(l)
        (q, cmp, sk, sv, win, gate, qmla, latkpe, mq, mk, mv) = inproj(xs, mods, g_mix, lw, c4s, s4s, tbl_spec_s, tm_s)
        s3 = lambda a: a.reshape(db, ds, a.shape[-1])
        q8, qmla8, mq8 = (_pad_rows(s3(a), DEC_Q) for a in (q, qmla, mq))
        new16 = lambda a: _pad_rows(s3(a), DEC_NEW)
        a_pool = cmp_chunk_proj_t(cmp_pool, l, page_table.reshape(-1), lw["w_chunks"])
        kcvc = cmp_assemble(a_pool.reshape(db, past_len // NSA_CMP_STRIDE, -1), lw["pe_const"], lw["w2_bd"])
        oc8, sel8 = nsa_cmp_select(q8, kcvc, bb=_largest_divisor(db, DEC_SEQS), tq=DEC_Q, t0=past_len,
                                   nc=past_len // NSA_CMP_STRIDE - 1,
                                   ns=past_len // NSA_SEL_BLOCK, own_in_range=False, slopes=nsa_sl)
        os8 = paged_attention(q8, sel8, new16(sk), None, new16(sv), slc_k_pool, None, slc_v_pool, page_table, l,
                              jobs=nsa_jobs, out_width=512, past_len=past_len, n_new=ds, k_t=True, v_t=True,
                              pages_per_step=DEC_PAGES_SMALL)
        ow8 = paged_attention(q8, None, new16(win), None, None, win_pool, None, None, ident_win, l,
                              jobs=win_jobs, out_width=512, past_len=past_len, n_new=ds, k_t=True,
                              key_pos0=past_len - win_len, window=NSA_WINDOW)
        lat_new = new16(latkpe)
        ctx8 = paged_attention(qmla8, None, lat_new[:, :, :LANE], lat_new[:, :, LANE:LANE + MLA_ROPE], None,
                               cache_mla_latent, kpe_pool, None, page_table, l,
                               jobs=mla_dec_jobs, out_width=512, past_len=past_len, n_new=ds, k_t=False, k2_t=True,
                               pages_per_step=DEC_PAGES_SMALL)
        kmean = moba_kmean_paged(moba_k_pool, l, page_table)
        msel8 = moba_select(mq8, kmean, tq=DEC_Q, t0=past_len, n_full=past_len // MOBA_BLOCK, own_in_range=False)
        om8 = paged_attention(mq8, msel8, new16(mk), None, new16(mv), moba_k_pool, None, moba_v_pool, page_table, l,
                              jobs=moba_jobs, out_width=256, past_len=past_len, n_new=ds, k_t=True, v_t=True)
        win_buf = state_nsa_win_kv[l].reshape(db, win_len, 2 * NSA_GROUPS * NSA_DH)
        win_cat = jnp.concatenate([win_buf, s3(win)], axis=1)
        d2 = lambda a: a[:, :ds].reshape(ts, a.shape[-1])
        x1 = merge(xs, mods, g_mix, d2(oc8), d2(os8), d2(ow8), gate, d2(ctx8), d2(om8), lw, tm_s)
        xs = ffn(x1, mods, g_ffn, lw, fin, last, tm_s)
        keep = min(NSA_WINDOW, past_len + ds)
        new_win = win_cat[:, -keep:].reshape(db, keep, 2, NSA_GROUPS, NSA_DH)
        st_s.append((cmp.reshape(db, ds, 2, NSA_GROUPS, NSA_DH), sk.reshape(db, ds, NSA_GROUPS, NSA_DH),
                     sv.reshape(db, ds, NSA_GROUPS, NSA_DH), new_win,
                     s3(latkpe)[:, :, :MLA_KV_LORA], s3(latkpe)[:, :, LANE:LANE + MLA_ROPE],
                     mk.reshape(db, ds, MOBA_HEADS, MOBA_DH), mv.reshape(db, ds, MOBA_HEADS, MOBA_DH)))

    outs_p = [jnp.stack(a, axis=0) for a in zip(*st_p)]
    outs_s = [jnp.stack(a, axis=0) for a in zip(*st_s)]
    return (xp.reshape(nb, seq, d), xs.reshape(db, ds, d), *outs_p, *outs_s)
```
